```python
import math, functools
import jax, jax.numpy as jnp
from jax import lax
import numpy as np

D_MODEL = 1024
BATCH = 8
SEQ = 2048
DEPTH = 1
DEC_BATCH = 128
DEC_SEQ = 1
PAST_LEN = 8192
PAGE_SIZE = 128

HEAD_DIM = 64
H_R = 8
H_F = 8
D_R = H_R * HEAD_DIM
D_F = H_F * HEAD_DIM
LORA_W = 64
LORA_A = 64
LORA_G = 160
D_FF = -(-8 * D_MODEL // (3 * 256)) * 256
Q_BLOCK = 128
ALPHA = (2 * DEPTH) ** 0.25
BETA = (8 * DEPTH) ** -0.25
LN_EPS = 1e-5
GN_EPS = 64e-5
NEG = -1e30
SCALE = HEAD_DIM ** -0.5
RWKV_COLS = 3 * D_R + LORA_W + LORA_A + LORA_G
FOX_COLS = 3 * D_F + H_F
GATE_COLS = 2 * D_MODEL
D_IN = RWKV_COLS + FOX_COLS + GATE_COLS

kernel_name = "rwkv7_fox_gated_hybrid_step"

F32 = jnp.float32


def split_cols(a, sizes):
    offs = np.cumsum(sizes)[:-1].tolist()
    return jnp.split(a, offs, axis=-1)


def layer_norm(x, g, b):
    xf = x.astype(F32)
    mu = jnp.mean(xf, -1, keepdims=True)
    var = jnp.mean(jnp.square(xf - mu), -1, keepdims=True)
    return ((xf - mu) * lax.rsqrt(var + LN_EPS) * g.astype(F32) + b.astype(F32)).astype(x.dtype)


def token_shift(p, prev):
    return jnp.concatenate([prev[:, None, :].astype(p.dtype), p[:, :-1, :]], axis=1)


def wkv7_scan(S0, r, log_w, k, v, kk, a):
    def step(S, inp):
        r_t, lw_t, k_t, v_t, kk_t, a_t = inp
        sa = jnp.einsum('bhvk,bhk->bhv', S, -kk_t)
        S = (S * jnp.exp(lw_t)[:, :, None, :]
             + jnp.einsum('bhv,bhk->bhvk', sa, kk_t * a_t)
             + jnp.einsum('bhv,bhk->bhvk', v_t, k_t))
        y = jnp.einsum('bhvk,bhk->bhv', S, r_t)
        return S, y
    xs = tuple(jnp.moveaxis(t.astype(F32), 1, 0) for t in (r, log_w, k, v, kk, a))
    S, ys = lax.scan(step, S0.astype(F32), xs)
    return jnp.moveaxis(ys, 0, 1), S


def rwkv7_branch(p_r, p_prev, wkv0, lp):
    B, T, _ = p_r.shape
    xx = p_r + (token_shift(p_r, p_prev) - p_r) * lp['mu']
    r, k, v, xw, xa, xg = split_cols(xx, [D_R, D_R, D_R, LORA_W, LORA_A, LORA_G])
    w = -jax.nn.softplus(-(lp['w0'] + jnp.tanh(xw) @ lp['w2'])) - 0.5
    log_w = -jnp.exp(w.astype(F32))
    a = jax.nn.sigmoid(lp['a0'] + xa @ lp['a2'])
    g = jax.nn.sigmoid(xg) @ lp['g2']
    heads = lambda t: t.reshape(B, T, H_R, HEAD_DIM)
    kk = heads(k * lp['k_k']).astype(F32)
    kk = kk * lax.rsqrt(jnp.maximum(jnp.sum(jnp.square(kk), -1, keepdims=True), 1e-24))
    k = k * (1 + (a - 1) * lp['k_a'])
    r, k, v, a, log_w = heads(r), heads(k), heads(v), heads(a), heads(log_w)
    y, wkv = wkv7_scan(wkv0, r, log_w, k, v, kk, a)
    mu = jnp.mean(y, -1, keepdims=True)
    var = jnp.mean(jnp.square(y - mu), -1, keepdims=True)
    yn = ((y - mu) * lax.rsqrt(var + GN_EPS)).reshape(B, T, D_R)
    yn = (yn * lp['lnx_g'].astype(F32) + lp['lnx_b'].astype(F32)).astype(p_r.dtype)
    bonus = (jnp.sum(r * k * lp['r_k'], -1, keepdims=True) * v).reshape(B, T, D_R)
    return (yn + bonus) * g, wkv


def fox_prompt_attend(q, k, v, logf):
    B, T, H, N = q.shape
    nb = T // Q_BLOCK
    c_k = jnp.transpose(jnp.cumsum(logf, axis=1), (0, 2, 1))
    qb = jnp.moveaxis(q.reshape(B, nb, Q_BLOCK, H, N), 1, 0)
    cb = jnp.moveaxis(c_k.reshape(B, H, nb, Q_BLOCK), 2, 0)
    key_pos = jnp.arange(T)

    def block(args):
        i, q_i, c_i = args
        s = jnp.einsum('bqhd,bkhd->bhqk', q_i, k).astype(F32) * SCALE
        s = s + c_i[..., None] - c_k[:, :, None, :]
        q_pos = i * Q_BLOCK + jnp.arange(Q_BLOCK)
        s = jnp.where(key_pos[None, :] <= q_pos[:, None], s, NEG)
        p = jax.nn.softmax(s, axis=-1).astype(v.dtype)
        return jnp.einsum('bhqk,bkhd->bqhd', p, v)

    o = lax.map(block, (jnp.arange(nb), qb, cb))
    return jnp.moveaxis(o, 0, 1).reshape(B, T, H * N)


def fox_sample_attend(q, k, v, logf, cache_k, cache_v, cache_logf, page_table, layer):
    Bd, T, H, N = q.shape
    lf_past = cache_logf[layer, page_table].reshape(Bd, -1, H).astype(F32)
    P = lf_past.shape[1]
    suffix = lax.cumsum(lf_past, axis=1, reverse=True) - lf_past
    c_q = jnp.transpose(jnp.cumsum(logf, axis=1), (0, 2, 1))
    k_past = cache_k[layer, page_table].reshape(Bd, P, H, N)
    s_past = (jnp.einsum('bqhd,bkhd->bhqk', q, k_past).astype(F32) * SCALE
              + c_q[..., None] + jnp.transpose(suffix, (0, 2, 1))[:, :, None, :])
    s_new = jnp.einsum('bqhd,bkhd->bhqk', q, k).astype(F32) * SCALE + c_q[..., None] - c_q[:, :, None, :]
    pos = jnp.arange(T)
    s_new = jnp.where(pos[None, :] <= pos[:, None], s_new, NEG)
    p = jax.nn.softmax(jnp.concatenate([s_past, s_new], axis=-1), axis=-1).astype(v.dtype)
    v_past = cache_v[layer, page_table].reshape(Bd, P, H, N)
    o = jnp.einsum('bhqk,bkhd->bqhd', p[..., :P], v_past) + jnp.einsum('bhqk,bkhd->bqhd', p[..., P:], v)
    return o.reshape(Bd, T, H * N)


def trunk_layer(x, shift_prev, wkv0, attend, lp):
    B, T, _ = x.shape
    p = x @ lp['w_in']
    p_r, p_f, p_g = split_cols(p, [RWKV_COLS, FOX_COLS, GATE_COLS])
    out_r, wkv = rwkv7_branch(p_r, shift_prev, wkv0, lp)
    q, k, v, f = split_cols(p_f, [D_F, D_F, D_F, H_F])
    hd = lambda t: t.reshape(B, T, H_F, HEAD_DIM)
    q, k, v = hd(q), hd(k), hd(v)
    logf = jax.nn.log_sigmoid((f + lp['b_f']).astype(F32))
    o_f = attend(q, k, v, logf)
    g_r, g_f = split_cols(p_g, [D_MODEL, D_MODEL])
    mixed = jax.nn.sigmoid(g_r) * (out_r @ lp['w_br']) + jax.nn.sigmoid(g_f) * (o_f @ lp['w_bf'])
    h = layer_norm(ALPHA * x + mixed @ lp['w_o'], lp['ln1_g'], lp['ln1_b'])
    u_gate, u_val = split_cols(h @ lp['w_up'], [D_FF, D_FF])
    y = layer_norm(ALPHA * h + (jax.nn.silu(u_gate) * u_val) @ lp['w_down'], lp['ln2_g'], lp['ln2_b'])
    return y, (k, v, logf.astype(x.dtype), wkv.astype(x.dtype), p_r[:, -1])


def setup_inputs(seed: int = 0) -> dict:
    key = jax.random.key(seed)
    ks = jax.random.split(key, 32)
    n_pages = PAST_LEN // PAGE_SIZE
    n_used = DEC_BATCH * n_pages
    n_pool = n_used + n_used // 4
    nrm = lambda k, shape, s=1.0: jax.random.normal(k, shape, F32) * s
    page_table = jax.random.permutation(ks[0], n_pool)[:n_used].reshape(DEC_BATCH, n_pages).astype(jnp.int32)
    return {
        'x_prompt': nrm(ks[1], (BATCH, SEQ, D_MODEL)),
        'x_sample': nrm(ks[2], (DEC_BATCH, DEC_SEQ, D_MODEL)),
        'cache_k': nrm(ks[3], (DEPTH, n_pool, PAGE_SIZE, H_F, HEAD_DIM)),
        'cache_v': nrm(ks[4], (DEPTH, n_pool, PAGE_SIZE, H_F, HEAD_DIM)),
        'cache_logf': jax.nn.log_sigmoid(8.5 + nrm(ks[5], (DEPTH, n_pool, PAGE_SIZE, H_F), 0.5)),
        'page_table': page_table,
        'state_wkv': nrm(ks[6], (DEPTH, DEC_BATCH, H_R, HEAD_DIM, HEAD_DIM)),
        'state_shift': nrm(ks[7], (DEPTH, DEC_BATCH, RWKV_COLS)),
        'w_in': nrm(ks[8], (DEPTH, D_MODEL, D_IN), D_MODEL ** -0.5),
        'mu': jax.random.uniform(ks[9], (DEPTH, RWKV_COLS), F32),
        'w0': jax.random.uniform(ks[10], (DEPTH, D_R), F32, -6.0, -1.0),
        'w2': nrm(ks[11], (DEPTH, LORA_W, D_R), 0.5 * LORA_W ** -0.5),
        'a0': nrm(ks[12], (DEPTH, D_R), 0.1),
        'a2': nrm(ks[13], (DEPTH, LORA_A, D_R), LORA_A ** -0.5),
        'g2': nrm(ks[14], (DEPTH, LORA_G, D_R), LORA_G ** -0.5),
        'k_k': 0.85 + nrm(ks[15], (DEPTH, D_R), 0.05),
        'k_a': 1.0 + nrm(ks[16], (DEPTH, D_R), 0.05),
        'r_k': nrm(ks[17], (DEPTH, H_R, HEAD_DIM), 0.1),
        'lnx_g': 1.0 + nrm(ks[18], (DEPTH, D_R), 0.02),
        'lnx_b': nrm(ks[19], (DEPTH, D_R), 0.02),
        'b_f': jax.random.uniform(ks[20], (DEPTH, H_F), F32, 6.0, 9.0),
        'w_br': nrm(ks[21], (DEPTH, D_R, D_MODEL), D_R ** -0.5),
        'w_bf': nrm(ks[22], (DEPTH, D_F, D_MODEL), D_F ** -0.5),
        'w_o': nrm(ks[23], (DEPTH, D_MODEL, D_MODEL), BETA * D_MODEL ** -0.5),
        'ln1_g': 1.0 + nrm(ks[24], (DEPTH, D_MODEL), 0.02),
        'ln1_b': nrm(ks[25], (DEPTH, D_MODEL), 0.02),
        'w_up': nrm(ks[26], (DEPTH, D_MODEL, 2 * D_FF), D_MODEL ** -0.5),
        'w_down': nrm(ks[27], (DEPTH, D_FF, D_MODEL), BETA * D_FF ** -0.5),
        'ln2_g': 1.0 + nrm(ks[28], (DEPTH, D_MODEL), 0.02),
        'ln2_b': nrm(ks[29], (DEPTH, D_MODEL), 0.02),
    }


def reference(x_prompt, x_sample, cache_k, cache_v, cache_logf, page_table, state_wkv, state_shift,
              w_in, mu, w0, w2, a0, a2, g2, k_k, k_a, r_k, lnx_g, lnx_b, b_f,
              w_br, w_bf, w_o, ln1_g, ln1_b, w_up, w_down, ln2_g, ln2_b):
    xp, xs = x_prompt, x_sample
    kp, vp, lfp, wp, sp = [], [], [], [], []
    kd, vd, lfd, wd, sd = [], [], [], [], []
    for l in range(DEPTH):
        lp = dict(w_in=w_in[l], mu=mu[l], w0=w0[l], w2=w2[l], a0=a0[l], a2=a2[l], g2=g2[l],
                  k_k=k_k[l], k_a=k_a[l], r_k=r_k[l], lnx_g=lnx_g[l], lnx_b=lnx_b[l], b_f=b_f[l],
                  w_br=w_br[l], w_bf=w_bf[l], w_o=w_o[l], ln1_g=ln1_g[l], ln1_b=ln1_b[l],
                  w_up=w_up[l], w_down=w_down[l], ln2_g=ln2_g[l], ln2_b=ln2_b[l])
        shift0 = jnp.zeros((xp.shape[0], RWKV_COLS), xp.dtype)
        wkv0 = jnp.zeros((xp.shape[0], H_R, HEAD_DIM, HEAD_DIM), F32)
        xp, (k_, v_, lf_, w_, s_) = trunk_layer(xp, shift0, wkv0, fox_prompt_attend, lp)
        kp.append(k_); vp.append(v_); lfp.append(lf_); wp.append(w_); sp.append(s_)
        attend_s = functools.partial(fox_sample_attend, cache_k=cache_k, cache_v=cache_v,
                                     cache_logf=cache_logf, page_table=page_table, layer=l)
        xs, (k_, v_, lf_, w_, s_) = trunk_layer(xs, state_shift[l], state_wkv[l], attend_s, lp)
        kd.append(k_); vd.append(v_); lfd.append(lf_); wd.append(w_); sd.append(s_)
    return (xp, xs,
            jnp.stack(kp), jnp.stack(vp), jnp.stack(lfp), jnp.stack(wp), jnp.stack(sp),
            jnp.stack(kd), jnp.stack(vd), jnp.stack(lfd), jnp.stack(wd), jnp.stack(sd))
```

```python
import functools

import jax
import jax.numpy as jnp
from jax import lax
from jax.experimental import pallas as pl
from jax.experimental.pallas import tpu as pltpu

F32 = jnp.float32
BF16 = jnp.bfloat16

D_MODEL = 1024
HEAD_DIM = 64
N_HEADS = 8
D_HEADS = N_HEADS * HEAD_DIM
N_PAIRS = N_HEADS // 2
LANES = 128
SUBLANES = 8
LORA_W, LORA_A, LORA_G = 64, 64, 160
RWKV_COLS = 3 * D_HEADS + LORA_W + LORA_A + LORA_G
D_FF = 2816
LN_EPS = 1e-5
GN_EPS = 64e-5
NEG = -1e30
SCALE = HEAD_DIM ** -0.5
ALPHA = 2.0 ** 0.25
CHUNK = 128
VMEM_LIMIT = 56 * 1024 * 1024

LORA_TILE = 512
F_OFF = 384
COL_GR, COL_GF, COL_R, COL_Q = 0, 1024, 2048, 4096
N_TOK = 4608
RW = 2048


def _dot(a, b):
    return jnp.dot(a, b, preferred_element_type=F32)


def _dot_nt(a, b):
    return lax.dot_general(a, b, (((1,), (1,)), ((), ())), preferred_element_type=F32)


def _dot_tn(a, b):
    return lax.dot_general(a, b, (((0,), (0,)), ((), ())), preferred_element_type=F32)


def _split3(x):
    hi = x.astype(BF16)
    r1 = x - hi.astype(F32)
    mid = r1.astype(BF16)
    lo = (r1 - mid.astype(F32)).astype(BF16)
    return hi, mid, lo


def _dot_x3(x, m):
    hi, mid, lo = _split3(x)
    return _dot(hi, m) + _dot(mid, m) + _dot(lo, m)


def _dot_3x(m, x):
    hi, mid, lo = _split3(x)
    return _dot(m, hi) + _dot(m, mid) + _dot(m, lo)


def _sigmoid(z):
    return 1.0 / (1.0 + jnp.exp(-z))


def _softplus(z):
    return jnp.maximum(z, 0.0) + jnp.log1p(jnp.exp(-jnp.abs(z)))


def _log_sigmoid(z):
    return -_softplus(-z)


def _cparams(sem):
    return pltpu.CompilerParams(dimension_semantics=sem, vmem_limit_bytes=VMEM_LIMIT)


def _const_spec(shape):
    nd = len(shape)
    return pl.BlockSpec(shape, lambda *_: (0,) * nd, pipeline_mode=pl.Buffered(1))


def _proj_kernel(x_ref, wtok_ref, wt_ref, tok_ref, t_ref, *, tn):
    xb = x_ref[...].astype(BF16)
    for j in range(wtok_ref.shape[0] // tn):
        tok_ref[:, j * tn:(j + 1) * tn] = _dot_nt(xb, wtok_ref[j * tn:(j + 1) * tn, :])
    t_ref[0] = _dot_nt(wt_ref[...], xb)


def _project(x, wtok, wt, nb, tm):
    m = x.shape[0]
    t = m // nb
    nt = t // tm
    return pl.pallas_call(
        functools.partial(_proj_kernel, tn=512),
        grid=(m // tm,),
        in_specs=[pl.BlockSpec((tm, D_MODEL), lambda i: (i, 0)),
                  _const_spec(wtok.shape), _const_spec(wt.shape)],
        out_specs=[pl.BlockSpec((tm, wtok.shape[0]), lambda i: (i, 0)),
                   pl.BlockSpec((1, wt.shape[0], tm), lambda i: (i // nt, 0, i % nt))],
        out_shape=[jax.ShapeDtypeStruct((m, wtok.shape[0]), F32),
                   jax.ShapeDtypeStruct((nb, wt.shape[0], t), F32)],
        compiler_params=_cparams(("arbitrary",)),
        name="proj",
    )(x, wtok, wt)


def _rwkv_prep(p, prev, mu, w0, a0, k_k, k_a, w2p, a2p, g2p, bones):
    xx = p + (prev - p) * mu
    r = xx[:, 0:512]
    k = xx[:, 512:1024]
    v = xx[:, 1024:1536]
    t0 = xx[:, 1536:1664]
    t12 = xx[:, 1664:1920]
    w = -_softplus(-(w0 + _dot(jnp.tanh(t0).astype(BF16), w2p))) - 0.5
    log_w = -jnp.exp(w)
    a = _sigmoid(a0 + _dot(t0.astype(BF16), a2p))
    g = _dot(_sigmoid(t12).astype(BF16), g2p)
    kk = k * k_k
    ss = _dot_x3(kk * kk, bones)
    kk = kk * lax.rsqrt(jnp.maximum(ss, 1e-24))
    k2 = k * (1.0 + (a - 1.0) * k_a)
    return r, k2, v, kk, a, log_w, g


def _rwkv_post(y, r, k2, v, g, r_k, lnx_g, lnx_b, bones):
    inv = 1.0 / HEAD_DIM
    mean = _dot_x3(y, bones) * inv
    d = y - mean
    var = _dot_x3(d * d, bones) * inv
    yn = d * lax.rsqrt(var + GN_EPS) * lnx_g + lnx_b
    bonus = _dot_x3(r * k2 * r_k, bones) * v
    return (yn + bonus) * g


def _rwkv_chunk_kernel(p_ref, mu_ref, w0_ref, a0_ref, kk_ref, ka_ref, rk_ref, lng_ref, lnb_ref,
                       w2_ref, a2_ref, g2_ref, bones_ref, tri_ref,
                       out_ref, wkv_ref, shift_ref,
                       prev_s, wl_s, st_s, at_s, rt_s, ar_s, rr_s, br_s, kr_s, v_s, be_s, ke_s, y_s):
    c = pl.program_id(1)
    L = CHUNK

    @pl.when(c == 0)
    def _():
        prev_s[...] = jnp.zeros_like(prev_s)
        st_s[...] = jnp.zeros_like(st_s)

    p = p_ref[...]
    row = lax.broadcasted_iota(jnp.int32, (L, RW), 0)
    prev = jnp.where(row == 0, prev_s[...], pltpu.roll(p, 1, axis=0))
    last = p_ref[L - 1:L, :]
    prev_s[...] = last
    bones = bones_ref[...]
    r, k2, v, kk, a, lw, g = _rwkv_prep(p, prev, mu_ref[...], w0_ref[...], a0_ref[...], kk_ref[...],
                                        ka_ref[...], w2_ref[...], a2_ref[...], g2_ref[...], bones)
    b = kk * a
    gam = _dot_3x(tri_ref[...], lw)
    y_s[...] = gam
    gmid = y_s[L // 2 - 1:L // 2, :]
    glast = y_s[L - 1:L, :]
    e_mid_inv = jnp.exp(-gmid)
    a_true = -kk * jnp.exp(gam - lw)
    r_true = r * jnp.exp(gam)
    e_mg = jnp.exp(gmid - gam)
    e_end = jnp.exp(glast - gam)
    wl_s[...] = jnp.exp(glast)
    at_s[...] = a_true
    rt_s[...] = r_true
    ar_s[...] = a_true * e_mid_inv
    rr_s[...] = r_true * e_mid_inv
    br_s[...] = b * e_mg
    kr_s[...] = k2 * e_mg
    v_s[...] = v
    be_s[...] = b * e_end
    ke_s[...] = k2 * e_end

    lane = lax.broadcasted_iota(jnp.int32, (L, LANES), 1)
    rowl = lax.broadcasted_iota(jnp.int32, (L, LANES), 0)
    lo = lane < HEAD_DIM
    strict = rowl > lane
    incl = rowl >= lane
    lo64 = lax.broadcasted_iota(jnp.int32, (HEAD_DIM, LANES), 1) < HEAD_DIM
    n_dbl = 7

    for pp in range(N_PAIRS):
        sl = slice(pp * LANES, (pp + 1) * LANES)
        a_t = at_s[:, sl]
        r_t = rt_s[:, sl]
        a_r = ar_s[:, sl]
        r_r = rr_s[:, sl]
        vb = v_s[:, sl].astype(BF16)
        rhs = jnp.concatenate([br_s[:, sl], kr_s[:, sl]], axis=0).astype(BF16)
        be = be_s[:, sl].astype(BF16)
        ke = ke_s[:, sl].astype(BF16)
        a_sw = pltpu.roll(a_t, HEAD_DIM, axis=1)
        r_sw = pltpu.roll(r_t, HEAD_DIM, axis=1)
        ry = []
        qq = []
        for hh in range(2):
            msk = lo if hh == 0 else jnp.logical_not(lo)
            lhs = jnp.concatenate([jnp.where(msk, a_r, 0.0), jnp.where(msk, r_r, 0.0)],
                                  axis=0).astype(BF16)
            gm = _dot_nt(lhs, rhs)
            aab = jnp.where(strict, gm[:L, :L], 0.0).astype(BF16)
            aak = jnp.where(strict, gm[:L, L:], 0.0).astype(BF16)
            arb = jnp.where(incl, gm[L:, :L], 0.0).astype(BF16)
            ark = jnp.where(incl, gm[L:, L:], 0.0).astype(BF16)
            av = _dot(aak, vb)
            x = jnp.where(lo, av, a_sw) if hh == 0 else jnp.where(lo, a_sw, av)
            pm = aab
            z = x
            for it in range(n_dbl):
                z = z + _dot(pm, z.astype(BF16))
                if it < n_dbl - 1:
                    pm = _dot(pm, pm).astype(BF16)
            zb = z.astype(BF16)
            avk = _dot(ark, vb)
            tail = jnp.where(lo, avk, r_sw) if hh == 0 else jnp.where(lo, r_sw, avk)
            ry.append(_dot(arb, zb) + tail)
            qq.append(_dot_tn(zb, be))
        vk = _dot_tn(vb, ke)
        rcomb = jnp.where(lo, ry[1], ry[0])
        y0 = jnp.where(lo, ry[0], ry[1])
        wl_p = wl_s[:, sl]
        mpair = jnp.concatenate([jnp.where(lo64, qq[0][HEAD_DIM:], 0.0),
                                 jnp.where(lo64, 0.0, qq[1][:HEAD_DIM])], axis=0)
        mpair = mpair + jnp.where(rowl == lane, wl_p, 0.0)
        npair = jnp.where(lo64, qq[0][:HEAD_DIM] + vk[:HEAD_DIM], qq[1][HEAD_DIM:] + vk[HEAD_DIM:])
        sp = st_s[pp]
        ssw = pltpu.roll(sp, HEAD_DIM, axis=1)
        santi = jnp.concatenate([jnp.where(lo64, 0.0, ssw), jnp.where(lo64, ssw, 0.0)], axis=0)
        y_s[:, sl] = _dot_nt(rcomb.astype(BF16), santi.astype(BF16)) + y0
        st_s[pp] = _dot(sp.astype(BF16), mpair.astype(BF16)) + npair

    out_ref[...] = _rwkv_post(y_s[...], r, k2, v, g, rk_ref[...], lng_ref[...], lnb_ref[...], bones)

    @pl.when(c == pl.num_programs(1) - 1)
    def _():
        wkv_ref[0] = st_s[...]
        shift_ref[0] = last


def _rwkv_prompt(tok, nb, par):
    m = tok.shape[0]
    nc = m // nb // CHUNK
    vec = pl.BlockSpec((1, D_HEADS), lambda b, c: (0, 0))
    return pl.pallas_call(
        _rwkv_chunk_kernel,
        grid=(nb, nc),
        in_specs=[pl.BlockSpec((CHUNK, RW), lambda b, c: (b * nc + c, COL_R // RW)),
                  pl.BlockSpec((1, RW), lambda b, c: (0, 0)),
                  vec, vec, vec, vec, vec, vec, vec,
                  pl.BlockSpec((LANES, D_HEADS), lambda b, c: (0, 0)),
                  pl.BlockSpec((LANES, D_HEADS), lambda b, c: (0, 0)),
                  pl.BlockSpec((2 * LANES, D_HEADS), lambda b, c: (0, 0)),
                  pl.BlockSpec((D_HEADS, D_HEADS), lambda b, c: (0, 0)),
                  pl.BlockSpec((CHUNK, CHUNK), lambda b, c: (0, 0))],
        out_specs=[pl.BlockSpec((CHUNK, D_HEADS), lambda b, c: (b * nc + c, 0)),
                   pl.BlockSpec((1, N_PAIRS, HEAD_DIM, LANES), lambda b, c: (b, 0, 0, 0)),
                   pl.BlockSpec((1, 1, RW), lambda b, c: (b, 0, 0))],
        out_shape=[jax.ShapeDtypeStruct((m, D_HEADS), F32),
                   jax.ShapeDtypeStruct((nb, N_PAIRS, HEAD_DIM, LANES), F32),
                   jax.ShapeDtypeStruct((nb, 1, RW), F32)],
        scratch_shapes=[pltpu.VMEM((1, RW), F32), pltpu.VMEM((1, D_HEADS), F32),
                        pltpu.VMEM((N_PAIRS, HEAD_DIM, LANES), F32)]
                       + [pltpu.VMEM((CHUNK, D_HEADS), F32)] * 10,
        compiler_params=_cparams(("arbitrary", "arbitrary")),
        name="rwkv_chunk",
    )(tok, par["mu"], par["w0"], par["a0"], par["k_k"], par["k_a"], par["r_k"], par["lnx_g"],
      par["lnx_b"], par["w2p"], par["a2p"], par["g2p"], par["bones"], par["tri"])


def _fox_gate_kernel(ft_ref, ftok_ref, bfc_ref, bfr_ref, ul_ref, ll_ref, lft_ref, ct_ref, ccol_ref):
    t = ft_ref.shape[2]
    lft = _log_sigmoid(ft_ref[0][:N_HEADS, :] + bfc_ref[:, 0:1])
    lft_ref[0] = lft
    carry = jnp.zeros((N_HEADS, LANES), F32)
    for blk in range(t // LANES):
        cs = _dot_x3(lft[:, blk * LANES:(blk + 1) * LANES], ul_ref[...])
        ct_ref[0, :, blk * LANES:(blk + 1) * LANES] = cs[:, :LANES] + carry
        carry = carry + cs[:, LANES:]
    carry_r = jnp.zeros((LANES, LANES), F32)
    for blk in range(t // LANES):
        lf = _log_sigmoid(ftok_ref[blk * LANES:(blk + 1) * LANES, :] + bfr_ref[...])
        cs = _dot_3x(ll_ref[...], lf)
        ccol_ref[blk * LANES:(blk + 1) * LANES, :] = cs[:LANES] + carry_r
        carry_r = carry_r + cs[LANES:]


def _fox_gate(tok, tt, nb, par):
    m = tok.shape[0]
    t = m // nb
    fcol = (COL_R + 3 * D_HEADS + F_OFF) // LANES
    frow = (2 * D_HEADS) // LANES
    return pl.pallas_call(
        _fox_gate_kernel,
        grid=(nb,),
        in_specs=[pl.BlockSpec((1, LANES, t), lambda b: (b, frow, 0)),
                  pl.BlockSpec((t, LANES), lambda b: (b, fcol)),
                  pl.BlockSpec((N_HEADS, LANES), lambda b: (0, 0)),
                  pl.BlockSpec((1, LANES), lambda b: (0, 0)),
                  pl.BlockSpec((LANES, 2 * LANES), lambda b: (0, 0)),
                  pl.BlockSpec((2 * LANES, LANES), lambda b: (0, 0))],
        out_specs=[pl.BlockSpec((1, N_HEADS, t), lambda b: (b, 0, 0)),
                   pl.BlockSpec((1, N_HEADS, t), lambda b: (b, 0, 0)),
                   pl.BlockSpec((t, LANES), lambda b: (b, 0))],
        out_shape=[jax.ShapeDtypeStruct((nb, N_HEADS, t), F32),
                   jax.ShapeDtypeStruct((nb, N_HEADS, t), F32),
                   jax.ShapeDtypeStruct((m, LANES), F32)],
        compiler_params=_cparams(("arbitrary",)),
        name="fox_gate",
    )(tt, tok, par["bf_col"], par["bf_row"], par["u_lane"], par["l_sub"])


def _fox_attn_kernel(q_ref, kt_ref, vt_ref, ct_ref, ccol_ref, o_ref, kb_s, vb_s, *, tq):
    qi = pl.program_id(1)

    @pl.when(qi == 0)
    def _():
        kb_s[...] = kt_ref[0].astype(BF16)
        vb_s[...] = vt_ref[0].astype(BF16)

    lane = lax.broadcasted_iota(jnp.int32, (tq, LANES), 1)
    lo = lane < HEAD_DIM
    rowq = lax.broadcasted_iota(jnp.int32, (tq, tq), 0)
    colq = lax.broadcasted_iota(jnp.int32, (tq, tq), 1)
    causal = colq <= rowq
    ccol = ccol_ref[...]

    for pp in range(N_PAIRS):
        rs = slice(pp * LANES, (pp + 1) * LANES)
        q = q_ref[:, rs] * SCALE
        qh = [jnp.where(lo, q, 0.0).astype(BF16), jnp.where(lo, 0.0, q).astype(BF16)]
        cq = [ccol[:, 2 * pp + hh:2 * pp + hh + 1] for hh in range(2)]

        def step(j, carry, masked):
            off = pl.multiple_of(j * tq, tq)
            kblk = kb_s[rs, pl.ds(off, tq)]
            vblk = vb_s[rs, pl.ds(off, tq)]
            new = []
            for hh in range(2):
                m_i, l_i, acc = carry[hh]
                ck = ct_ref[0, 2 * pp + hh:2 * pp + hh + 1, pl.ds(off, tq)]
                s = _dot(qh[hh], kblk) + cq[hh] - ck
                if masked:
                    s = jnp.where(causal, s, NEG)
                m_new = jnp.maximum(m_i, jnp.max(s, axis=1, keepdims=True))
                alpha = jnp.exp(m_i - m_new)
                pe = jnp.exp(s - m_new)
                l_new = l_i * alpha + jnp.sum(pe, axis=1, keepdims=True)
                acc = acc * alpha + _dot_nt(pe.astype(BF16), vblk)
                new.append((m_new, l_new, acc))
            return tuple(new)

        init = tuple((jnp.full((tq, 1), NEG, F32), jnp.zeros((tq, 1), F32),
                      jnp.zeros((tq, LANES), F32)) for _ in range(2))
        carry = lax.fori_loop(0, qi, lambda j, cr: step(j, cr, False), init)
        carry = step(qi, carry, True)
        oa = carry[0][2] / carry[0][1]
        ob = carry[1][2] / carry[1][1]
        o_ref[:, rs] = jnp.where(lo, oa, ob)


def _fox_attn(tok, tt, ct, ccol, nb, tq):
    m = tok.shape[0]
    t = m // nb
    nq = t // tq
    return pl.pallas_call(
        functools.partial(_fox_attn_kernel, tq=tq),
        grid=(nb, nq),
        in_specs=[pl.BlockSpec((tq, D_HEADS), lambda b, i: (b * nq + i, COL_Q // D_HEADS)),
                  pl.BlockSpec((1, D_HEADS, t), lambda b, i: (b, 0, 0)),
                  pl.BlockSpec((1, D_HEADS, t), lambda b, i: (b, 1, 0)),
                  pl.BlockSpec((1, N_HEADS, t), lambda b, i: (b, 0, 0)),
                  pl.BlockSpec((tq, LANES), lambda b, i: (b * nq + i, 0))],
        out_specs=pl.BlockSpec((tq, D_HEADS), lambda b, i: (b * nq + i, 0)),
        out_shape=jax.ShapeDtypeStruct((m, D_HEADS), F32),
        scratch_shapes=[pltpu.VMEM((D_HEADS, t), BF16), pltpu.VMEM((D_HEADS, t), BF16)],
        compiler_params=_cparams(("arbitrary", "arbitrary")),
        name="fox_attn",
    )(tok, tt, tt, ct, ccol)


def _layer_norm(z, g, b):
    mu = jnp.mean(z, axis=-1, keepdims=True)
    d = z - mu
    var = jnp.mean(d * d, axis=-1, keepdims=True)
    return d * lax.rsqrt(var + LN_EPS) * g + b


def _merge_kernel(x_ref, gr_ref, gf_ref, r_ref, f_ref, wbr_ref, wbf_ref, wo_ref, g_ref, b_ref, h_ref):
    a = _dot(r_ref[...].astype(BF16), wbr_ref[...])
    b = _dot(f_ref[...].astype(BF16), wbf_ref[...])
    mixed = _sigmoid(gr_ref[...]) * a + _sigmoid(gf_ref[...]) * b
    z = ALPHA * x_ref[...] + _dot(mixed.astype(BF16), wo_ref[...])
    h_ref[...] = _layer_norm(z, g_ref[...], b_ref[...])


def _merge(x, tok, out_r, o_f, par, tm):
    m = x.shape[0]
    row = lambda c: pl.BlockSpec((tm, c), lambda i: (i, 0))
    return pl.pallas_call(
        _merge_kernel,
        grid=(m // tm,),
        in_specs=[row(D_MODEL),
                  pl.BlockSpec((tm, D_MODEL), lambda i: (i, COL_GR // D_MODEL)),
                  pl.BlockSpec((tm, D_MODEL), lambda i: (i, COL_GF // D_MODEL)),
                  row(D_HEADS), row(D_HEADS),
                  _const_spec((D_HEADS, D_MODEL)), _const_spec((D_HEADS, D_MODEL)),
                  _const_spec((D_MODEL, D_MODEL)), _const_spec((1, D_MODEL)), _const_spec((1, D_MODEL))],
        out_specs=row(D_MODEL),
        out_shape=jax.ShapeDtypeStruct((m, D_MODEL), F32),
        compiler_params=_cparams(("arbitrary",)),
        name="merge",
    )(x, tok, tok, out_r, o_f, par["w_br"], par["w_bf"], par["w_o"], par["ln1_g"], par["ln1_b"])


FF_CHUNK = D_FF // 2


def _ffn_kernel(h_ref, wup_ref, wdn_ref, g_ref, b_ref, y_ref):
    h = h_ref[...]
    hb = h.astype(BF16)
    acc = ALPHA * h
    for ci in range(D_FF // FF_CHUNK):
        cs = slice(ci * FF_CHUNK, (ci + 1) * FF_CHUNK)
        ug = _dot(hb, wup_ref[:, cs])
        uv = _dot(hb, wup_ref[:, D_FF + ci * FF_CHUNK:D_FF + (ci + 1) * FF_CHUNK])
        act = (ug * _sigmoid(ug) * uv).astype(BF16)
        acc = acc + _dot(act, wdn_ref[cs, :])
    y_ref[...] = _layer_norm(acc, g_ref[...], b_ref[...])


def _ffn(h, par, tm):
    m = h.shape[0]
    return pl.pallas_call(
        _ffn_kernel,
        grid=(m // tm,),
        in_specs=[pl.BlockSpec((tm, D_MODEL), lambda i: (i, 0)),
                  _const_spec((D_MODEL, 2 * D_FF)), _const_spec((D_FF, D_MODEL)),
                  _const_spec((1, D_MODEL)), _const_spec((1, D_MODEL))],
        out_specs=pl.BlockSpec((tm, D_MODEL), lambda i: (i, 0)),
        out_shape=jax.ShapeDtypeStruct((m, D_MODEL), F32),
        compiler_params=_cparams(("arbitrary",)),
        name="ffn",
    )(h, par["w_up"], par["w_down"], par["ln2_g"], par["ln2_b"])


def _rwkv_step_kernel(p_ref, prev_ref, mu_ref, w0_ref, a0_ref, kk_ref, ka_ref, rk_ref, lng_ref, lnb_ref,
                      w2_ref, a2_ref, g2_ref, bones_ref, s_ref,
                      out_ref, so_ref,
                      r_s, k_s, v_s, g_s, rt_s, kt_s, vt_s, nk_s, bt_s, dt_s, yt_s):
    h = pl.program_id(0)

    @pl.when(h == 0)
    def _():
        r, k2, v, kk, a, lw, g = _rwkv_prep(p_ref[...], prev_ref[...], mu_ref[...], w0_ref[...],
                                            a0_ref[...], kk_ref[...], ka_ref[...], w2_ref[...],
                                            a2_ref[...], g2_ref[...], bones_ref[...])
        r_s[...] = r
        k_s[...] = k2
        v_s[...] = v
        g_s[...] = g
        rt_s[...] = r.T
        kt_s[...] = k2.T
        vt_s[...] = v.T
        nk_s[...] = (-kk).T
        bt_s[...] = (kk * a).T
        dt_s[...] = jnp.exp(lw).T

    hs = pl.ds(pl.multiple_of(h * HEAD_DIM, HEAD_DIM), HEAD_DIM)
    nkk = nk_s[hs, :]
    dec = dt_s[hs, :]
    bb = bt_s[hs, :]
    kk2 = kt_s[hs, :]
    rr = rt_s[hs, :]

    def body(vi, _):
        s = s_ref[0, 0, vi]
        sa = jnp.sum(s * nkk, axis=0, keepdims=True)
        vrow = vt_s[pl.ds(h * HEAD_DIM + vi, 1), :]
        s2 = s * dec + sa * bb + vrow * kk2
        so_ref[0, 0, vi] = s2
        yt_s[pl.ds(h * HEAD_DIM + vi, 1), :] = jnp.sum(s2 * rr, axis=0, keepdims=True)
        return 0

    lax.fori_loop(0, HEAD_DIM, body, 0)

    @pl.when(h == N_HEADS - 1)
    def _():
        out_ref[...] = _rwkv_post(yt_s[...].T, r_s[...], k_s[...], v_s[...], g_s[...], rk_ref[...],
                                  lng_ref[...], lnb_ref[...], bones_ref[...])


def _rwkv_sample(tok, prev, state_t, par):
    nb = tok.shape[0]
    vec = pl.BlockSpec((1, D_HEADS), lambda h: (0, 0))
    sspec = pl.BlockSpec((1, 1, HEAD_DIM, HEAD_DIM, nb), lambda h: (0, h, 0, 0, 0))
    return pl.pallas_call(
        _rwkv_step_kernel,
        grid=(N_HEADS,),
        in_specs=[pl.BlockSpec((nb, RW), lambda h: (0, COL_R // RW)),
                  pl.BlockSpec((nb, RW), lambda h: (0, 0)),
                  pl.BlockSpec((1, RW), lambda h: (0, 0)),
                  vec, vec, vec, vec, vec, vec, vec,
                  pl.BlockSpec((LANES, D_HEADS), lambda h: (0, 0)),
                  pl.BlockSpec((LANES, D_HEADS), lambda h: (0, 0)),
                  pl.BlockSpec((2 * LANES, D_HEADS), lambda h: (0, 0)),
                  pl.BlockSpec((D_HEADS, D_HEADS), lambda h: (0, 0)),
                  sspec],
        out_specs=[pl.BlockSpec((nb, D_HEADS), lambda h: (0, 0)), sspec],
        out_shape=[jax.ShapeDtypeStruct((nb, D_HEADS), F32),
                   jax.ShapeDtypeStruct(state_t.shape, F32)],
        scratch_shapes=[pltpu.VMEM((nb, D_HEADS), F32)] * 4 + [pltpu.VMEM((D_HEADS, nb), F32)] * 7,
        compiler_params=_cparams(("arbitrary",)),
        name="rwkv_step",
    )(tok, prev, par["mu"], par["w0"], par["a0"], par["k_k"], par["k_a"], par["r_k"], par["lnx_g"],
      par["lnx_b"], par["w2p"], par["a2p"], par["g2p"], par["bones"], state_t)


PAGE_GROUP = 8


def _allreduce_sublanes(x):
    x = x + pltpu.roll(x, 4, axis=0)
    x = x + pltpu.roll(x, 2, axis=0)
    return x + pltpu.roll(x, 1, axis=0)


def _fold8(x):
    return jnp.sum(x.reshape(HEAD_DIM // SUBLANES, SUBLANES, LANES), axis=0)


def _tile8(x):
    return jnp.broadcast_to(x[None], (HEAD_DIM // SUBLANES, SUBLANES, LANES)).reshape(HEAD_DIM, LANES)


def _fox_decode_kernel(pt_ref, qt_ref, kn_ref, vn_ref, ft_ref, bfc_ref, us_ref, *rest, n_groups):
    pg = PAGE_GROUP
    k_refs = rest[:pg]
    v_refs = rest[pg:2 * pg]
    lf_refs = rest[2 * pg:3 * pg]
    o_ref, lfn_ref, qb_s, m_s, l_s, c_s, acc_s, ot_s = rest[3 * pg:]
    b = pl.program_id(0)
    j = pl.program_id(1)
    nb = pl.num_programs(0)
    rowsel = lax.broadcasted_iota(jnp.int32, (LANES, LANES), 0)
    lanes8 = lax.broadcasted_iota(jnp.int32, (SUBLANES, LANES), 1)
    lfn_all = _log_sigmoid(ft_ref[0][:N_HEADS, :] + bfc_ref[:, 0:1])

    @pl.when(j == 0)
    def _():
        sel = (rowsel == b).astype(BF16)
        qb_s[...] = _dot_x3(qt_ref[0], sel) * SCALE
        cq = _dot_x3(lfn_all, sel)
        for hh in range(N_HEADS):
            c_s[hh] = jnp.broadcast_to(cq[hh:hh + 1, :], (SUBLANES, LANES))
        m_s[...] = jnp.full_like(m_s, NEG)
        l_s[...] = jnp.zeros_like(l_s)
        acc_s[...] = jnp.zeros_like(acc_s)

    @pl.when(jnp.logical_and(b == 0, j == 0))
    def _():
        lfn_ref[...] = lfn_all
        ot_s[...] = jnp.zeros_like(ot_s)

    for i in range(pg - 1, -1, -1):
        su = _dot_x3(lf_refs[i][0, 0], us_ref[...])
        for hh in range(N_HEADS):
            hs = slice(hh * HEAD_DIM, (hh + 1) * HEAD_DIM)
            carry = c_s[hh]
            bias = jnp.broadcast_to(su[hh:hh + 1, :LANES], (SUBLANES, LANES)) + carry
            s = _allreduce_sublanes(_fold8(k_refs[i][0, 0, hh] * qb_s[hs, :])) + bias
            m_old = m_s[hh]
            m_new = jnp.maximum(m_old, s)
            alpha = jnp.exp(m_old - m_new)
            pe = jnp.exp(s - m_new)
            m_s[hh] = m_new
            l_s[hh] = l_s[hh] * alpha + pe
            acc_s[hs, :] = acc_s[hs, :] * _tile8(alpha) + v_refs[i][0, 0, hh] * _tile8(pe)
            c_s[hh] = carry + jnp.broadcast_to(su[hh:hh + 1, LANES:], (SUBLANES, LANES))

    @pl.when(j == n_groups - 1)
    def _():
        sel = (rowsel == b).astype(BF16)
        knb = _dot_x3(kn_ref[0], sel)
        vnb = _dot_x3(vn_ref[0], sel)
        for hh in range(N_HEADS):
            hs = slice(hh * HEAD_DIM, (hh + 1) * HEAD_DIM)
            m = m_s[hh]
            mx = jnp.max(m, axis=1, keepdims=True)
            wgt = jnp.exp(m - mx)
            l_tot = jnp.sum(l_s[hh] * wgt, axis=1, keepdims=True)
            o_col = jnp.sum(acc_s[hs, :] * _tile8(wgt), axis=1, keepdims=True)
            s_new = _allreduce_sublanes(_fold8(knb[hs, :] * qb_s[hs, :]))
            m2 = jnp.maximum(mx, s_new)
            e1 = jnp.exp(mx - m2)
            e2 = jnp.exp(s_new - m2)
            inv = 1.0 / (l_tot * e1 + e2)
            o_h = (o_col * _tile8(e1) + vnb[hs, :] * _tile8(e2)) * _tile8(inv)
            lane64 = lax.broadcasted_iota(jnp.int32, (HEAD_DIM, LANES), 1)
            ot_s[hs, :] = jnp.where(lane64 == b, o_h, ot_s[hs, :])

    @pl.when(jnp.logical_and(b == nb - 1, j == n_groups - 1))
    def _():
        o_ref[...] = ot_s[...].T


def _fox_decode(tt, ck, cv, clf, page_table, par):
    nb = tt.shape[2]
    n_pages = page_table.shape[1]
    pg = PAGE_GROUP
    ng = n_pages // pg

    def page_spec(shape, i):
        nd = len(shape)
        return pl.BlockSpec((1, 1) + shape,
                            lambda b, j, pt: (0, pt[b, (ng - 1 - j) * pg + i]) + (0,) * nd)

    trow = lambda r, n: pl.BlockSpec((1, n, nb), lambda b, j, pt: (0, r, 0))
    in_specs = [trow(0, D_HEADS), trow(1, D_HEADS), trow(2, D_HEADS),
                pl.BlockSpec((1, LANES, nb), lambda b, j, pt: (0, 3 * D_HEADS // LANES, 0)),
                pl.BlockSpec((N_HEADS, LANES), lambda b, j, pt: (0, 0)),
                pl.BlockSpec((LANES, 2 * LANES), lambda b, j, pt: (0, 0))]
    in_specs += [page_spec((N_HEADS, HEAD_DIM, LANES), i) for i in range(pg)]
    in_specs += [page_spec((N_HEADS, HEAD_DIM, LANES), i) for i in range(pg)]
    in_specs += [page_spec((N_HEADS, LANES), i) for i in range(pg)]
    return pl.pallas_call(
        functools.partial(_fox_decode_kernel, n_groups=ng),
        grid_spec=pltpu.PrefetchScalarGridSpec(
            num_scalar_prefetch=1,
            grid=(nb, ng),
            in_specs=in_specs,
            out_specs=[pl.BlockSpec((nb, D_HEADS), lambda b, j, pt: (0, 0)),
                       pl.BlockSpec((N_HEADS, nb), lambda b, j, pt: (0, 0))],
            scratch_shapes=[pltpu.VMEM((D_HEADS, LANES), F32),
                            pltpu.VMEM((N_HEADS, SUBLANES, LANES), F32),
                            pltpu.VMEM((N_HEADS, SUBLANES, LANES), F32),
                            pltpu.VMEM((N_HEADS, SUBLANES, LANES), F32),
                            pltpu.VMEM((D_HEADS, LANES), F32),
                            pltpu.VMEM((D_HEADS, nb), F32)]),
        out_shape=[jax.ShapeDtypeStruct((nb, D_HEADS), F32),
                   jax.ShapeDtypeStruct((N_HEADS, nb), F32)],
        compiler_params=_cparams(("arbitrary", "arbitrary")),
        name="fox_decode",
    )(page_table, tt, tt, tt, tt, par["bf_col"], par["u_suffix"], *([ck] * pg), *([cv] * pg), *([clf] * pg))


def _prepare_params(w_in, mu, w0, w2, a0, a2, g2, k_k, k_a, r_k, lnx_g, lnx_b, b_f,
                    w_br, w_bf, w_o, ln1_g, ln1_b, w_up, w_down, ln2_g, ln2_b):
    wt = w_in.T
    o = 0
    seg = {}
    for name, n in (("r", 512), ("k", 512), ("v", 512), ("xw", LORA_W), ("xa", LORA_A), ("xg", LORA_G),
                    ("q", 512), ("kf", 512), ("vf", 512), ("f", N_HEADS), ("gr", 1024), ("gf", 1024)):
        seg[name] = wt[o:o + n]
        o += n
    z = lambda n: jnp.zeros((n, D_MODEL), F32)
    lora = jnp.concatenate([seg["xw"], seg["xa"], seg["xg"], z(F_OFF - 288), seg["f"],
                            z(LORA_TILE - F_OFF - N_HEADS)], axis=0)
    fpad = jnp.concatenate([seg["f"], z(LANES - N_HEADS)], axis=0)
    wtok = jnp.concatenate([seg["gr"], seg["gf"], seg["r"], seg["k"], seg["v"], lora, seg["q"]], axis=0)
    wt_prompt = jnp.concatenate([seg["kf"], seg["vf"], fpad], axis=0)
    wt_sample = jnp.concatenate([seg["q"], seg["kf"], seg["vf"], fpad], axis=0)
    row = lambda x: x.reshape(1, -1).astype(F32)
    hid = jnp.arange(D_HEADS) // HEAD_DIM
    i128 = jnp.arange(LANES)
    upper_incl = (i128[:, None] <= i128[None, :])
    lower_incl = (i128[:, None] >= i128[None, :])
    ones = jnp.ones((LANES, LANES), bool)
    return dict(
        wtok=wtok.astype(BF16), wt_prompt=wt_prompt.astype(BF16), wt_sample=wt_sample.astype(BF16),
        mu=jnp.pad(row(mu), ((0, 0), (0, RW - RWKV_COLS))),
        w0=row(w0), a0=row(a0), k_k=row(k_k), k_a=row(k_a), r_k=row(r_k), lnx_g=row(lnx_g), lnx_b=row(lnx_b),
        w2p=jnp.pad(w2, ((0, LANES - LORA_W), (0, 0))).astype(BF16),
        a2p=jnp.pad(a2, ((LORA_W, 0), (0, 0))).astype(BF16),
        g2p=jnp.pad(g2, ((0, 2 * LANES - LORA_G), (0, 0))).astype(BF16),
        bones=(hid[:, None] == hid[None, :]).astype(BF16),
        tri=lower_incl.astype(BF16),
        bf_col=jnp.broadcast_to(b_f.reshape(N_HEADS, 1), (N_HEADS, LANES)).astype(F32),
        bf_row=jnp.pad(row(b_f), ((0, 0), (0, LANES - N_HEADS))),
        u_lane=jnp.concatenate([upper_incl, ones], axis=1).astype(BF16),
        l_sub=jnp.concatenate([lower_incl, ones], axis=0).astype(BF16),
        u_suffix=jnp.concatenate([i128[:, None] > i128[None, :], ones], axis=1).astype(BF16),
        w_br=w_br.astype(BF16), w_bf=w_bf.astype(BF16), w_o=w_o.astype(BF16),
        ln1_g=row(ln1_g), ln1_b=row(ln1_b), w_up=w_up.astype(BF16), w_down=w_down.astype(BF16),
        ln2_g=row(ln2_g), ln2_b=row(ln2_b))


def _unpack_shift(rows):
    return rows[:, :RWKV_COLS]


def _prompt_group(x_prompt, par):
    nb, t, _ = x_prompt.shape
    x = x_prompt.reshape(nb * t, D_MODEL)
    tok, tt = _project(x, par["wtok"], par["wt_prompt"], nb, 256)
    out_r, wkv, shift = _rwkv_prompt(tok, nb, par)
    lft, ct, ccol = _fox_gate(tok, tt, nb, par)
    o_f = _fox_attn(tok, tt, ct, ccol, nb, 256)
    h = _merge(x, tok, out_r, o_f, par, 256)
    y = _ffn(h, par, 256)
    kt = tt[:, 0:D_HEADS].reshape(nb, N_HEADS, HEAD_DIM, t)
    vt = tt[:, D_HEADS:2 * D_HEADS].reshape(nb, N_HEADS, HEAD_DIM, t)
    k_out = jnp.transpose(kt, (0, 3, 1, 2))[None]
    v_out = jnp.transpose(vt, (0, 3, 1, 2))[None]
    lf_out = jnp.transpose(lft, (0, 2, 1))[None]
    wkv_out = jnp.transpose(wkv.reshape(nb, N_PAIRS, HEAD_DIM, 2, HEAD_DIM), (0, 1, 3, 2, 4))
    wkv_out = wkv_out.reshape(1, nb, N_HEADS, HEAD_DIM, HEAD_DIM)
    return (y.reshape(nb, t, D_MODEL), k_out, v_out, lf_out, wkv_out,
            _unpack_shift(shift.reshape(nb, RW))[None])


def _sample_group(x_sample, cache_k, cache_v, cache_logf, page_table, state_wkv, state_shift, par):
    nb = x_sample.shape[0]
    x = x_sample.reshape(nb, D_MODEL)
    tok, tt = _project(x, par["wtok"], par["wt_sample"], 1, nb)
    prev = jnp.pad(state_shift, ((0, 0), (0, RW - RWKV_COLS)))
    state_t = jnp.transpose(state_wkv, (1, 2, 3, 0))[None]
    out_r, state_o = _rwkv_sample(tok, prev, state_t, par)
    ck = jnp.transpose(cache_k, (0, 2, 3, 1))[None]
    cv = jnp.transpose(cache_v, (0, 2, 3, 1))[None]
    clf = jnp.transpose(cache_logf, (0, 2, 1))[None]
    o_f, lfn = _fox_decode(tt, ck, cv, clf, page_table, par)
    h = _merge(x, tok, out_r, o_f, par, nb)
    y = _ffn(h, par, nb)
    kt = tt[0, D_HEADS:2 * D_HEADS].reshape(N_HEADS, HEAD_DIM, nb)
    vt = tt[0, 2 * D_HEADS:3 * D_HEADS].reshape(N_HEADS, HEAD_DIM, nb)
    k_out = jnp.transpose(kt, (2, 0, 1)).reshape(1, nb, 1, N_HEADS, HEAD_DIM)
    v_out = jnp.transpose(vt, (2, 0, 1)).reshape(1, nb, 1, N_HEADS, HEAD_DIM)
    lf_out = jnp.transpose(lfn, (1, 0)).reshape(1, nb, 1, N_HEADS)
    wkv_out = jnp.transpose(state_o[0], (3, 0, 1, 2))[None]
    shift_out = tok[:, COL_R:COL_R + RWKV_COLS][None]
    return y.reshape(nb, 1, D_MODEL), k_out, v_out, lf_out, wkv_out, shift_out


def kernel(x_prompt, x_sample, cache_k, cache_v, cache_logf, page_table, state_wkv, state_shift, w_in, mu, w0, w2, a0, a2, g2, k_k, k_a, r_k, lnx_g, lnx_b, b_f, w_br, w_bf, w_o, ln1_g, ln1_b, w_up, w_down, ln2_g, ln2_b):
    assert w_in.shape[0] == 1, "single-layer trunk"
    par = _prepare_params(w_in[0], mu[0], w0[0], w2[0], a0[0], a2[0], g2[0], k_k[0], k_a[0],
                          r_k[0].reshape(-1), lnx_g[0], lnx_b[0], b_f[0], w_br[0], w_bf[0], w_o[0],
                          ln1_g[0], ln1_b[0], w_up[0], w_down[0], ln2_g[0], ln2_b[0])
    yp, kp, vp, lfp, wp, sp = _prompt_group(x_prompt, par)
    ys, kd, vd, lfd, wd, sd = _sample_group(x_sample, cache_k[0], cache_v[0], cache_logf[0], page_table,
                                            state_wkv[0], state_shift[0], par)
    return (yp, ys, kp, vp, lfp, wp, sp, kd, vd, lfd, wd, sd)
```

```python
import functools

import jax
import jax.numpy as jnp
from jax import lax
from jax.experimental import pallas as pl
from jax.experimental.pallas import tpu as pltpu

F32 = jnp.float32
BF16 = jnp.bfloat16

D_MODEL = 1024
HEAD_DIM = 64
N_HEADS = 8
D_HEADS = N_HEADS * HEAD_DIM
N_PAIRS = N_HEADS // 2
LANES = 128
SUBLANES = 8
LORA_W, LORA_A, LORA_G = 64, 64, 160
RWKV_COLS = 3 * D_HEADS + LORA_W + LORA_A + LORA_G
D_FF = 2816
LN_EPS = 1e-5
GN_EPS = 64e-5
NEG = -1e30
SCALE = HEAD_DIM ** -0.5
ALPHA = 2.0 ** 0.25
CHUNK = 128
VMEM_LIMIT = 56 * 1024 * 1024

LORA_TILE = 512
F_OFF = 384
COL_GR, COL_GF, COL_R, COL_Q = 0, 1024, 2048, 4096
N_TOK = 4608
RW = 2048


def _dot(a, b):
    return jnp.dot(a, b, preferred_element_type=F32)


def _dot_nt(a, b):
    return lax.dot_general(a, b, (((1,), (1,)), ((), ())), preferred_element_type=F32)


def _dot_tn(a, b):
    return lax.dot_general(a, b, (((0,), (0,)), ((), ())), preferred_element_type=F32)


def _split3(x):
    hi = x.astype(BF16)
    r1 = x - hi.astype(F32)
    mid = r1.astype(BF16)
    lo = (r1 - mid.astype(F32)).astype(BF16)
    return hi, mid, lo


def _dot_x3(x, m):
    hi, mid, lo = _split3(x)
    return _dot(hi, m) + _dot(mid, m) + _dot(lo, m)


def _dot_3x(m, x):
    hi, mid, lo = _split3(x)
    return _dot(m, hi) + _dot(m, mid) + _dot(m, lo)


def _sigmoid(z):
    return 1.0 / (1.0 + jnp.exp(-z))


def _softplus(z):
    return jnp.maximum(z, 0.0) + jnp.log1p(jnp.exp(-jnp.abs(z)))


def _log_sigmoid(z):
    return -_softplus(-z)


def _cparams(sem):
    return pltpu.CompilerParams(dimension_semantics=sem, vmem_limit_bytes=VMEM_LIMIT)


def _const_spec(shape):
    nd = len(shape)
    return pl.BlockSpec(shape, lambda *_: (0,) * nd, pipeline_mode=pl.Buffered(1))


def _proj_kernel(x_ref, wtok_ref, wt_ref, tok_ref, t_ref, *, tn):
    xb = x_ref[...].astype(BF16)
    for j in range(wtok_ref.shape[0] // tn):
        tok_ref[:, j * tn:(j + 1) * tn] = _dot_nt(xb, wtok_ref[j * tn:(j + 1) * tn, :])
    t_ref[0] = _dot_nt(wt_ref[...], xb)


def _project(x, wtok, wt, nb, tm):
    m = x.shape[0]
    t = m // nb
    nt = t // tm
    return pl.pallas_call(
        functools.partial(_proj_kernel, tn=512),
        grid=(m // tm,),
        in_specs=[pl.BlockSpec((tm, D_MODEL), lambda i: (i, 0)),
                  _const_spec(wtok.shape), _const_spec(wt.shape)],
        out_specs=[pl.BlockSpec((tm, wtok.shape[0]), lambda i: (i, 0)),
                   pl.BlockSpec((1, wt.shape[0], tm), lambda i: (i // nt, 0, i % nt))],
        out_shape=[jax.ShapeDtypeStruct((m, wtok.shape[0]), F32),
                   jax.ShapeDtypeStruct((nb, wt.shape[0], t), F32)],
        compiler_params=_cparams(("arbitrary",)),
        name="proj",
    )(x, wtok, wt)


def _rwkv_prep(p, prev, mu, w0, a0, k_k, k_a, w2p, a2p, g2p, bones):
    xx = p + (prev - p) * mu
    r = xx[:, 0:512]
    k = xx[:, 512:1024]
    v = xx[:, 1024:1536]
    t0 = xx[:, 1536:1664]
    t12 = xx[:, 1664:1920]
    w = -_softplus(-(w0 + _dot(jnp.tanh(t0).astype(BF16), w2p))) - 0.5
    log_w = -jnp.exp(w)
    a = _sigmoid(a0 + _dot(t0.astype(BF16), a2p))
    g = _dot(_sigmoid(t12).astype(BF16), g2p)
    kk = k * k_k
    ss = _dot_x3(kk * kk, bones)
    kk = kk * lax.rsqrt(jnp.maximum(ss, 1e-24))
    k2 = k * (1.0 + (a - 1.0) * k_a)
    return r, k2, v, kk, a, log_w, g


def _rwkv_post(y, r, k2, v, g, r_k, lnx_g, lnx_b, bones):
    inv = 1.0 / HEAD_DIM
    mean = _dot_x3(y, bones) * inv
    d = y - mean
    var = _dot_x3(d * d, bones) * inv
    yn = d * lax.rsqrt(var + GN_EPS) * lnx_g + lnx_b
    bonus = _dot_x3(r * k2 * r_k, bones) * v
    return (yn + bonus) * g


def _rwkv_chunk_kernel(p_ref, mu_ref, w0_ref, a0_ref, kk_ref, ka_ref, rk_ref, lng_ref, lnb_ref,
                       w2_ref, a2_ref, g2_ref, bones_ref, tri_ref,
                       out_ref, wkv_ref, shift_ref,
                       prev_s, wl_s, st_s, at_s, rt_s, ar_s, rr_s, br_s, kr_s, v_s, be_s, ke_s, y_s):
    c = pl.program_id(1)
    L = CHUNK

    @pl.when(c == 0)
    def _():
        prev_s[...] = jnp.zeros_like(prev_s)
        st_s[...] = jnp.zeros_like(st_s)

    p = p_ref[...]
    row = lax.broadcasted_iota(jnp.int32, (L, RW), 0)
    prev = jnp.where(row == 0, prev_s[...], pltpu.roll(p, 1, axis=0))
    last = p_ref[L - 1:L, :]
    prev_s[...] = last
    bones = bones_ref[...]
    r, k2, v, kk, a, lw, g = _rwkv_prep(p, prev, mu_ref[...], w0_ref[...], a0_ref[...], kk_ref[...],
                                        ka_ref[...], w2_ref[...], a2_ref[...], g2_ref[...], bones)
    b = kk * a
    gam = _dot_3x(tri_ref[...], lw)
    y_s[...] = gam
    gmid = y_s[L // 2 - 1:L // 2, :]
    glast = y_s[L - 1:L, :]
    e_mid_inv = jnp.exp(-gmid)
    a_true = -kk * jnp.exp(gam - lw)
    r_true = r * jnp.exp(gam)
    e_mg = jnp.exp(gmid - gam)
    e_end = jnp.exp(glast - gam)
    wl_s[...] = jnp.exp(glast)
    at_s[...] = a_true
    rt_s[...] = r_true
    ar_s[...] = a_true * e_mid_inv
    rr_s[...] = r_true * e_mid_inv
    br_s[...] = b * e_mg
    kr_s[...] = k2 * e_mg
    v_s[...] = v
    be_s[...] = b * e_end
    ke_s[...] = k2 * e_end

    lane = lax.broadcasted_iota(jnp.int32, (L, LANES), 1)
    rowl = lax.broadcasted_iota(jnp.int32, (L, LANES), 0)
    lo = lane < HEAD_DIM
    strict = rowl > lane
    incl = rowl >= lane
    lo64 = lax.broadcasted_iota(jnp.int32, (HEAD_DIM, LANES), 1) < HEAD_DIM
    n_dbl = 7

    pairs = []
    for pp in range(N_PAIRS):
        sl = slice(pp * LANES, (pp + 1) * LANES)
        a_t = at_s[:, sl]
        r_t = rt_s[:, sl]
        pairs.append(dict(
            sl=sl, a_r=ar_s[:, sl], r_r=rr_s[:, sl], vb=v_s[:, sl].astype(BF16),
            rhs=jnp.concatenate([br_s[:, sl], kr_s[:, sl]], axis=0).astype(BF16),
            be=be_s[:, sl].astype(BF16), ke=ke_s[:, sl].astype(BF16),
            a_sw=pltpu.roll(a_t, HEAD_DIM, axis=1), r_sw=pltpu.roll(r_t, HEAD_DIM, axis=1)))
    heads = [(pp, hh) for pp in range(N_PAIRS) for hh in range(2)]
    gms = []
    for pp, hh in heads:
        d = pairs[pp]
        msk = lo if hh == 0 else jnp.logical_not(lo)
        lhs = jnp.concatenate([jnp.where(msk, d["a_r"], 0.0), jnp.where(msk, d["r_r"], 0.0)],
                              axis=0).astype(BF16)
        gms.append(_dot_nt(lhs, d["rhs"]))
    pms = [jnp.where(strict, gm[:L, :L], 0.0).astype(BF16) for gm in gms]
    aaks = [jnp.where(strict, gm[:L, L:], 0.0).astype(BF16) for gm in gms]
    arbs = [jnp.where(incl, gm[L:, :L], 0.0).astype(BF16) for gm in gms]
    arks = [jnp.where(incl, gm[L:, L:], 0.0).astype(BF16) for gm in gms]
    zs = []
    for i, (pp, hh) in enumerate(heads):
        d = pairs[pp]
        av = _dot(aaks[i], d["vb"])
        zs.append(jnp.where(lo, av, d["a_sw"]) if hh == 0 else jnp.where(lo, d["a_sw"], av))
    for it in range(n_dbl):
        zs = [z + _dot(pm, z.astype(BF16)) for z, pm in zip(zs, pms)]
        if it < n_dbl - 1:
            pms = [_dot(pm, pm).astype(BF16) for pm in pms]
    zbs = [z.astype(BF16) for z in zs]
    rys = []
    qqs = []
    for i, (pp, hh) in enumerate(heads):
        d = pairs[pp]
        avk = _dot(arks[i], d["vb"])
        tail = jnp.where(lo, avk, d["r_sw"]) if hh == 0 else jnp.where(lo, d["r_sw"], avk)
        rys.append(_dot(arbs[i], zbs[i]) + tail)
        qqs.append(_dot_tn(zbs[i], d["be"]))
    for pp in range(N_PAIRS):
        d = pairs[pp]
        sl = d["sl"]
        ry = rys[2 * pp:2 * pp + 2]
        qq = qqs[2 * pp:2 * pp + 2]
        vk = _dot_tn(d["vb"], d["ke"])
        rcomb = jnp.where(lo, ry[1], ry[0])
        y0 = jnp.where(lo, ry[0], ry[1])
        wl_p = wl_s[:, sl]
        mpair = jnp.concatenate([jnp.where(lo64, qq[0][HEAD_DIM:], 0.0),
                                 jnp.where(lo64, 0.0, qq[1][:HEAD_DIM])], axis=0)
        mpair = mpair + jnp.where(rowl == lane, wl_p, 0.0)
        npair = jnp.where(lo64, qq[0][:HEAD_DIM] + vk[:HEAD_DIM], qq[1][HEAD_DIM:] + vk[HEAD_DIM:])
        sp = st_s[pp]
        ssw = pltpu.roll(sp, HEAD_DIM, axis=1)
        santi = jnp.concatenate([jnp.where(lo64, 0.0, ssw), jnp.where(lo64, ssw, 0.0)], axis=0)
        y_s[:, sl] = _dot_nt(rcomb.astype(BF16), santi.astype(BF16)) + y0
        st_s[pp] = _dot(sp.astype(BF16), mpair.astype(BF16)) + npair

    out_ref[...] = _rwkv_post(y_s[...], r, k2, v, g, rk_ref[...], lng_ref[...], lnb_ref[...], bones)

    @pl.when(c == pl.num_programs(1) - 1)
    def _():
        wkv_ref[0] = st_s[...]
        shift_ref[0] = last


def _rwkv_prompt(tok, nb, par):
    m = tok.shape[0]
    nc = m // nb // CHUNK
    vec = pl.BlockSpec((1, D_HEADS), lambda b, c: (0, 0))
    return pl.pallas_call(
        _rwkv_chunk_kernel,
        grid=(nb, nc),
        in_specs=[pl.BlockSpec((CHUNK, RW), lambda b, c: (b * nc + c, COL_R // RW)),
                  pl.BlockSpec((1, RW), lambda b, c: (0, 0)),
                  vec, vec, vec, vec, vec, vec, vec,
                  pl.BlockSpec((LANES, D_HEADS), lambda b, c: (0, 0)),
                  pl.BlockSpec((LANES, D_HEADS), lambda b, c: (0, 0)),
                  pl.BlockSpec((2 * LANES, D_HEADS), lambda b, c: (0, 0)),
                  pl.BlockSpec((D_HEADS, D_HEADS), lambda b, c: (0, 0)),
                  pl.BlockSpec((CHUNK, CHUNK), lambda b, c: (0, 0))],
        out_specs=[pl.BlockSpec((CHUNK, D_HEADS), lambda b, c: (b * nc + c, 0)),
                   pl.BlockSpec((1, N_PAIRS, HEAD_DIM, LANES), lambda b, c: (b, 0, 0, 0)),
                   pl.BlockSpec((1, 1, RW), lambda b, c: (b, 0, 0))],
        out_shape=[jax.ShapeDtypeStruct((m, D_HEADS), F32),
                   jax.ShapeDtypeStruct((nb, N_PAIRS, HEAD_DIM, LANES), F32),
                   jax.ShapeDtypeStruct((nb, 1, RW), F32)],
        scratch_shapes=[pltpu.VMEM((1, RW), F32), pltpu.VMEM((1, D_HEADS), F32),
                        pltpu.VMEM((N_PAIRS, HEAD_DIM, LANES), F32)]
                       + [pltpu.VMEM((CHUNK, D_HEADS), F32)] * 10,
        compiler_params=_cparams(("arbitrary", "arbitrary")),
        name="rwkv_chunk",
    )(tok, par["mu"], par["w0"], par["a0"], par["k_k"], par["k_a"], par["r_k"], par["lnx_g"],
      par["lnx_b"], par["w2p"], par["a2p"], par["g2p"], par["bones"], par["tri"])


def _fox_gate_kernel(ft_ref, ftok_ref, bfc_ref, bfr_ref, ul_ref, ll_ref, lft_ref, ct_ref, ccol_ref):
    t = ft_ref.shape[2]
    lft = _log_sigmoid(ft_ref[0][:N_HEADS, :] + bfc_ref[:, 0:1])
    lft_ref[0] = lft
    carry = jnp.zeros((N_HEADS, LANES), F32)
    for blk in range(t // LANES):
        cs = _dot_x3(lft[:, blk * LANES:(blk + 1) * LANES], ul_ref[...])
        ct_ref[0, :, blk * LANES:(blk + 1) * LANES] = cs[:, :LANES] + carry
        carry = carry + cs[:, LANES:]
    carry_r = jnp.zeros((LANES, LANES), F32)
    for blk in range(t // LANES):
        lf = _log_sigmoid(ftok_ref[blk * LANES:(blk + 1) * LANES, :] + bfr_ref[...])
        cs = _dot_3x(ll_ref[...], lf)
        ccol_ref[blk * LANES:(blk + 1) * LANES, :] = cs[:LANES] + carry_r
        carry_r = carry_r + cs[LANES:]


def _fox_gate(tok, tt, nb, par):
    m = tok.shape[0]
    t = m // nb
    fcol = (COL_R + 3 * D_HEADS + F_OFF) // LANES
    frow = (2 * D_HEADS) // LANES
    return pl.pallas_call(
        _fox_gate_kernel,
        grid=(nb,),
        in_specs=[pl.BlockSpec((1, LANES, t), lambda b: (b, frow, 0)),
                  pl.BlockSpec((t, LANES), lambda b: (b, fcol)),
                  pl.BlockSpec((N_HEADS, LANES), lambda b: (0, 0)),
                  pl.BlockSpec((1, LANES), lambda b: (0, 0)),
                  pl.BlockSpec((LANES, 2 * LANES), lambda b: (0, 0)),
                  pl.BlockSpec((2 * LANES, LANES), lambda b: (0, 0))],
        out_specs=[pl.BlockSpec((1, N_HEADS, t), lambda b: (b, 0, 0)),
                   pl.BlockSpec((1, N_HEADS, t), lambda b: (b, 0, 0)),
                   pl.BlockSpec((t, LANES), lambda b: (b, 0))],
        out_shape=[jax.ShapeDtypeStruct((nb, N_HEADS, t), F32),
                   jax.ShapeDtypeStruct((nb, N_HEADS, t), F32),
                   jax.ShapeDtypeStruct((m, LANES), F32)],
        compiler_params=_cparams(("arbitrary",)),
        name="fox_gate",
    )(tt, tok, par["bf_col"], par["bf_row"], par["u_lane"], par["l_sub"])


def _fox_attn_kernel(q_ref, kt_ref, vt_ref, ct_ref, ccol_ref, o_ref,
                     kb_s, vb_s, qh_s, cq_s, m_s, l_s, acc_s, *, tq):
    qi = pl.program_id(1)

    @pl.when(qi == 0)
    def _():
        kb_s[...] = kt_ref[0].astype(BF16)
        vb_s[...] = vt_ref[0].astype(BF16)

    lane = lax.broadcasted_iota(jnp.int32, (tq, LANES), 1)
    lo = lane < HEAD_DIM
    rowq = lax.broadcasted_iota(jnp.int32, (tq, tq), 0)
    colq = lax.broadcasted_iota(jnp.int32, (tq, tq), 1)
    causal = colq <= rowq
    ccol = ccol_ref[...]

    for pp in range(N_PAIRS):
        q = q_ref[:, pp * LANES:(pp + 1) * LANES] * SCALE
        qh_s[2 * pp] = jnp.where(lo, q, 0.0).astype(BF16)
        qh_s[2 * pp + 1] = jnp.where(lo, 0.0, q).astype(BF16)
    for h in range(N_HEADS):
        cq_s[h] = ccol[:, h:h + 1]
    m_s[...] = jnp.full_like(m_s, NEG)
    l_s[...] = jnp.zeros_like(l_s)
    acc_s[...] = jnp.zeros_like(acc_s)

    def step(j, masked):
        off = pl.multiple_of(j * tq, tq)
        ss = []
        for h in range(N_HEADS):
            rs = slice((h // 2) * LANES, (h // 2 + 1) * LANES)
            s = _dot(qh_s[h], kb_s[rs, pl.ds(off, tq)]) + cq_s[h] - ct_ref[0, h:h + 1, pl.ds(off, tq)]
            ss.append(jnp.where(causal, s, NEG) if masked else s)
        m_old = [m_s[h] for h in range(N_HEADS)]
        m_new = [jnp.maximum(m_old[h], jnp.max(ss[h], axis=1, keepdims=True)) for h in range(N_HEADS)]
        alpha = [jnp.exp(m_old[h] - m_new[h]) for h in range(N_HEADS)]
        pes = [jnp.exp(ss[h] - m_new[h]) for h in range(N_HEADS)]
        for h in range(N_HEADS):
            m_s[h] = m_new[h]
            l_s[h] = l_s[h] * alpha[h] + jnp.sum(pes[h], axis=1, keepdims=True)
        for h in range(N_HEADS):
            rs = slice((h // 2) * LANES, (h // 2 + 1) * LANES)
            acc_s[h] = acc_s[h] * alpha[h] + _dot_nt(pes[h].astype(BF16), vb_s[rs, pl.ds(off, tq)])

    def body(j, carry):
        step(j, False)
        return carry

    lax.fori_loop(0, qi, body, 0)
    step(qi, True)
    for pp in range(N_PAIRS):
        oa = acc_s[2 * pp] / l_s[2 * pp]
        ob = acc_s[2 * pp + 1] / l_s[2 * pp + 1]
        o_ref[:, pp * LANES:(pp + 1) * LANES] = jnp.where(lo, oa, ob)


def _fox_attn(tok, tt, ct, ccol, nb, tq):
    m = tok.shape[0]
    t = m // nb
    nq = t // tq
    return pl.pallas_call(
        functools.partial(_fox_attn_kernel, tq=tq),
        grid=(nb, nq),
        in_specs=[pl.BlockSpec((tq, D_HEADS), lambda b, i: (b * nq + i, COL_Q // D_HEADS)),
                  pl.BlockSpec((1, D_HEADS, t), lambda b, i: (b, 0, 0)),
                  pl.BlockSpec((1, D_HEADS, t), lambda b, i: (b, 1, 0)),
                  pl.BlockSpec((1, N_HEADS, t), lambda b, i: (b, 0, 0)),
                  pl.BlockSpec((tq, LANES), lambda b, i: (b * nq + i, 0))],
        out_specs=pl.BlockSpec((tq, D_HEADS), lambda b, i: (b * nq + i, 0)),
        out_shape=jax.ShapeDtypeStruct((m, D_HEADS), F32),
        scratch_shapes=[pltpu.VMEM((D_HEADS, t), BF16), pltpu.VMEM((D_HEADS, t), BF16),
                        pltpu.VMEM((N_HEADS, tq, LANES), BF16), pltpu.VMEM((N_HEADS, tq, 1), F32),
                        pltpu.VMEM((N_HEADS, tq, 1), F32), pltpu.VMEM((N_HEADS, tq, 1), F32),
                        pltpu.VMEM((N_HEADS, tq, LANES), F32)],
        compiler_params=_cparams(("arbitrary", "arbitrary")),
        name="fox_attn",
    )(tok, tt, tt, ct, ccol)


def _layer_norm(z, g, b):
    mu = jnp.mean(z, axis=-1, keepdims=True)
    d = z - mu
    var = jnp.mean(d * d, axis=-1, keepdims=True)
    return d * lax.rsqrt(var + LN_EPS) * g + b


def _merge_kernel(x_ref, gr_ref, gf_ref, r_ref, f_ref, wbr_ref, wbf_ref, wo_ref, g_ref, b_ref, h_ref):
    a = _dot(r_ref[...].astype(BF16), wbr_ref[...])
    b = _dot(f_ref[...].astype(BF16), wbf_ref[...])
    mixed = _sigmoid(gr_ref[...]) * a + _sigmoid(gf_ref[...]) * b
    z = ALPHA * x_ref[...] + _dot(mixed.astype(BF16), wo_ref[...])
    h_ref[...] = _layer_norm(z, g_ref[...], b_ref[...])


def _merge(x, tok, out_r, o_f, par, tm):
    m = x.shape[0]
    row = lambda c: pl.BlockSpec((tm, c), lambda i: (i, 0))
    return pl.pallas_call(
        _merge_kernel,
        grid=(m // tm,),
        in_specs=[row(D_MODEL),
                  pl.BlockSpec((tm, D_MODEL), lambda i: (i, COL_GR // D_MODEL)),
                  pl.BlockSpec((tm, D_MODEL), lambda i: (i, COL_GF // D_MODEL)),
                  row(D_HEADS), row(D_HEADS),
                  _const_spec((D_HEADS, D_MODEL)), _const_spec((D_HEADS, D_MODEL)),
                  _const_spec((D_MODEL, D_MODEL)), _const_spec((1, D_MODEL)), _const_spec((1, D_MODEL))],
        out_specs=row(D_MODEL),
        out_shape=jax.ShapeDtypeStruct((m, D_MODEL), F32),
        compiler_params=_cparams(("arbitrary",)),
        name="merge",
    )(x, tok, tok, out_r, o_f, par["w_br"], par["w_bf"], par["w_o"], par["ln1_g"], par["ln1_b"])


FF_CHUNK = D_FF // 2


def _ffn_kernel(h_ref, wup_ref, wdn_ref, g_ref, b_ref, y_ref):
    h = h_ref[...]
    hb = h.astype(BF16)
    acc = ALPHA * h
    for ci in range(D_FF // FF_CHUNK):
        cs = slice(ci * FF_CHUNK, (ci + 1) * FF_CHUNK)
        ug = _dot(hb, wup_ref[:, cs])
        uv = _dot(hb, wup_ref[:, D_FF + ci * FF_CHUNK:D_FF + (ci + 1) * FF_CHUNK])
        act = (ug * _sigmoid(ug) * uv).astype(BF16)
        acc = acc + _dot(act, wdn_ref[cs, :])
    y_ref[...] = _layer_norm(acc, g_ref[...], b_ref[...])


def _ffn(h, par, tm):
    m = h.shape[0]
    return pl.pallas_call(
        _ffn_kernel,
        grid=(m // tm,),
        in_specs=[pl.BlockSpec((tm, D_MODEL), lambda i: (i, 0)),
                  _const_spec((D_MODEL, 2 * D_FF)), _const_spec((D_FF, D_MODEL)),
                  _const_spec((1, D_MODEL)), _const_spec((1, D_MODEL))],
        out_specs=pl.BlockSpec((tm, D_MODEL), lambda i: (i, 0)),
        out_shape=jax.ShapeDtypeStruct((m, D_MODEL), F32),
        compiler_params=_cparams(("arbitrary",)),
        name="ffn",
    )(h, par["w_up"], par["w_down"], par["ln2_g"], par["ln2_b"])


def _rwkv_step_kernel(p_ref, prev_ref, mu_ref, w0_ref, a0_ref, kk_ref, ka_ref, rk_ref, lng_ref, lnb_ref,
                      w2_ref, a2_ref, g2_ref, bones_ref, s_ref,
                      out_ref, so_ref,
                      r_s, k_s, v_s, g_s, rt_s, kt_s, vt_s, nk_s, bt_s, dt_s, yt_s):
    h = pl.program_id(0)

    @pl.when(h == 0)
    def _():
        r, k2, v, kk, a, lw, g = _rwkv_prep(p_ref[...], prev_ref[...], mu_ref[...], w0_ref[...],
                                            a0_ref[...], kk_ref[...], ka_ref[...], w2_ref[...],
                                            a2_ref[...], g2_ref[...], bones_ref[...])
        r_s[...] = r
        k_s[...] = k2
        v_s[...] = v
        g_s[...] = g
        rt_s[...] = r.T
        kt_s[...] = k2.T
        vt_s[...] = v.T
        nk_s[...] = (-kk).T
        bt_s[...] = (kk * a).T
        dt_s[...] = jnp.exp(lw).T

    hs = pl.ds(pl.multiple_of(h * HEAD_DIM, HEAD_DIM), HEAD_DIM)
    nkk = nk_s[hs, :]
    dec = dt_s[hs, :]
    bb = bt_s[hs, :]
    kk2 = kt_s[hs, :]
    rr = rt_s[hs, :]

    def body(vi, _):
        s = s_ref[0, 0, vi]
        sa = jnp.sum(s * nkk, axis=0, keepdims=True)
        vrow = vt_s[pl.ds(h * HEAD_DIM + vi, 1), :]
        s2 = s * dec + sa * bb + vrow * kk2
        so_ref[0, 0, vi] = s2
        yt_s[pl.ds(h * HEAD_DIM + vi, 1), :] = jnp.sum(s2 * rr, axis=0, keepdims=True)
        return 0

    lax.fori_loop(0, HEAD_DIM, body, 0)

    @pl.when(h == N_HEADS - 1)
    def _():
        out_ref[...] = _rwkv_post(yt_s[...].T, r_s[...], k_s[...], v_s[...], g_s[...], rk_ref[...],
                                  lng_ref[...], lnb_ref[...], bones_ref[...])


def _rwkv_sample(tok, prev, state_t, par):
    nb = tok.shape[0]
    vec = pl.BlockSpec((1, D_HEADS), lambda h: (0, 0))
    sspec = pl.BlockSpec((1, 1, HEAD_DIM, HEAD_DIM, nb), lambda h: (0, h, 0, 0, 0))
    return pl.pallas_call(
        _rwkv_step_kernel,
        grid=(N_HEADS,),
        in_specs=[pl.BlockSpec((nb, RW), lambda h: (0, COL_R // RW)),
                  pl.BlockSpec((nb, RW), lambda h: (0, 0)),
                  pl.BlockSpec((1, RW), lambda h: (0, 0)),
                  vec, vec, vec, vec, vec, vec, vec,
                  pl.BlockSpec((LANES, D_HEADS), lambda h: (0, 0)),
                  pl.BlockSpec((LANES, D_HEADS), lambda h: (0, 0)),
                  pl.BlockSpec((2 * LANES, D_HEADS), lambda h: (0, 0)),
                  pl.BlockSpec((D_HEADS, D_HEADS), lambda h: (0, 0)),
                  sspec],
        out_specs=[pl.BlockSpec((nb, D_HEADS), lambda h: (0, 0)), sspec],
        out_shape=[jax.ShapeDtypeStruct((nb, D_HEADS), F32),
                   jax.ShapeDtypeStruct(state_t.shape, F32)],
        scratch_shapes=[pltpu.VMEM((nb, D_HEADS), F32)] * 4 + [pltpu.VMEM((D_HEADS, nb), F32)] * 7,
        compiler_params=_cparams(("arbitrary",)),
        name="rwkv_step",
    )(tok, prev, par["mu"], par["w0"], par["a0"], par["k_k"], par["k_a"], par["r_k"], par["lnx_g"],
      par["lnx_b"], par["w2p"], par["a2p"], par["g2p"], par["bones"], state_t)


PAGE_GROUP = 8


def _allreduce_sublanes(x):
    x = x + pltpu.roll(x, 4, axis=0)
    x = x + pltpu.roll(x, 2, axis=0)
    return x + pltpu.roll(x, 1, axis=0)


def _fold8(x):
    return jnp.sum(x.reshape(HEAD_DIM // SUBLANES, SUBLANES, LANES), axis=0)


def _tile8(x):
    return jnp.broadcast_to(x[None], (HEAD_DIM // SUBLANES, SUBLANES, LANES)).reshape(HEAD_DIM, LANES)


def _fox_decode_kernel(pt_ref, qt_ref, kn_ref, vn_ref, ft_ref, bfc_ref, us_ref, *rest, n_groups):
    pg = PAGE_GROUP
    k_refs = rest[:pg]
    v_refs = rest[pg:2 * pg]
    lf_refs = rest[2 * pg:3 * pg]
    o_ref, lfn_ref, qb_s, m_s, l_s, c_s, acc_s, ot_s = rest[3 * pg:]
    b = pl.program_id(0)
    j = pl.program_id(1)
    nb = pl.num_programs(0)
    rowsel = lax.broadcasted_iota(jnp.int32, (LANES, LANES), 0)
    lanes8 = lax.broadcasted_iota(jnp.int32, (SUBLANES, LANES), 1)
    lfn_all = _log_sigmoid(ft_ref[0][:N_HEADS, :] + bfc_ref[:, 0:1])

    @pl.when(j == 0)
    def _():
        sel = (rowsel == b).astype(BF16)
        qb_s[...] = _dot_x3(qt_ref[0], sel) * SCALE
        cq = _dot_x3(lfn_all, sel)
        for hh in range(N_HEADS):
            c_s[hh] = jnp.broadcast_to(cq[hh:hh + 1, :], (SUBLANES, LANES))
        m_s[...] = jnp.full_like(m_s, NEG)
        l_s[...] = jnp.zeros_like(l_s)
        acc_s[...] = jnp.zeros_like(acc_s)

    @pl.when(jnp.logical_and(b == 0, j == 0))
    def _():
        lfn_ref[...] = lfn_all
        ot_s[...] = jnp.zeros_like(ot_s)

    su_all = _dot_x3(jnp.concatenate([lf_refs[i][0, 0] for i in range(pg)], axis=0), us_ref[...])
    sus = [su_all[i * N_HEADS:(i + 1) * N_HEADS] for i in range(pg)]
    for hh in range(N_HEADS):
        hs = slice(hh * HEAD_DIM, (hh + 1) * HEAD_DIM)
        qh = qb_s[hs, :]
        run = c_s[hh]
        scores = [None] * pg
        for i in range(pg - 1, -1, -1):
            bias = jnp.broadcast_to(sus[i][hh:hh + 1, :LANES], (SUBLANES, LANES)) + run
            scores[i] = _allreduce_sublanes(_fold8(k_refs[i][0, 0, hh] * qh)) + bias
            run = run + jnp.broadcast_to(sus[i][hh:hh + 1, LANES:], (SUBLANES, LANES))
        c_s[hh] = run
        m_old = m_s[hh]
        m_new = m_old
        for s in scores:
            m_new = jnp.maximum(m_new, s)
        alpha = jnp.exp(m_old - m_new)
        pes = [jnp.exp(s - m_new) for s in scores]
        m_s[hh] = m_new
        l_new = l_s[hh] * alpha
        acc = acc_s[hs, :] * _tile8(alpha)
        for i in range(pg):
            l_new = l_new + pes[i]
            acc = acc + v_refs[i][0, 0, hh] * _tile8(pes[i])
        l_s[hh] = l_new
        acc_s[hs, :] = acc

    @pl.when(j == n_groups - 1)
    def _():
        sel = (rowsel == b).astype(BF16)
        knb = _dot_x3(kn_ref[0], sel)
        vnb = _dot_x3(vn_ref[0], sel)
        for hh in range(N_HEADS):
            hs = slice(hh * HEAD_DIM, (hh + 1) * HEAD_DIM)
            m = m_s[hh]
            mx = jnp.max(m, axis=1, keepdims=True)
            wgt = jnp.exp(m - mx)
            l_tot = jnp.sum(l_s[hh] * wgt, axis=1, keepdims=True)
            o_col = jnp.sum(acc_s[hs, :] * _tile8(wgt), axis=1, keepdims=True)
            s_new = _allreduce_sublanes(_fold8(knb[hs, :] * qb_s[hs, :]))
            m2 = jnp.maximum(mx, s_new)
            e1 = jnp.exp(mx - m2)
            e2 = jnp.exp(s_new - m2)
            inv = 1.0 / (l_tot * e1 + e2)
            o_h = (o_col * _tile8(e1) + vnb[hs, :] * _tile8(e2)) * _tile8(inv)
            lane64 = lax.broadcasted_iota(jnp.int32, (HEAD_DIM, LANES), 1)
            ot_s[hs, :] = jnp.where(lane64 == b, o_h, ot_s[hs, :])

    @pl.when(jnp.logical_and(b == nb - 1, j == n_groups - 1))
    def _():
        o_ref[...] = ot_s[...].T


def _fox_decode(tt, ck, cv, clf, page_table, par):
    nb = tt.shape[2]
    n_pages = page_table.shape[1]
    pg = PAGE_GROUP
    ng = n_pages // pg

    def page_spec(shape, i):
        nd = len(shape)
        return pl.BlockSpec((1, 1) + shape,
                            lambda b, j, pt: (0, pt[b, (ng - 1 - j) * pg + i]) + (0,) * nd)

    trow = lambda r, n: pl.BlockSpec((1, n, nb), lambda b, j, pt: (0, r, 0))
    in_specs = [trow(0, D_HEADS), trow(1, D_HEADS), trow(2, D_HEADS),
                pl.BlockSpec((1, LANES, nb), lambda b, j, pt: (0, 3 * D_HEADS // LANES, 0)),
                pl.BlockSpec((N_HEADS, LANES), lambda b, j, pt: (0, 0)),
                pl.BlockSpec((LANES, 2 * LANES), lambda b, j, pt: (0, 0))]
    in_specs += [page_spec((N_HEADS, HEAD_DIM, LANES), i) for i in range(pg)]
    in_specs += [page_spec((N_HEADS, HEAD_DIM, LANES), i) for i in range(pg)]
    in_specs += [page_spec((N_HEADS, LANES), i) for i in range(pg)]
    return pl.pallas_call(
        functools.partial(_fox_decode_kernel, n_groups=ng),
        grid_spec=pltpu.PrefetchScalarGridSpec(
            num_scalar_prefetch=1,
            grid=(nb, ng),
            in_specs=in_specs,
            out_specs=[pl.BlockSpec((nb, D_HEADS), lambda b, j, pt: (0, 0)),
                       pl.BlockSpec((N_HEADS, nb), lambda b, j, pt: (0, 0))],
            scratch_shapes=[pltpu.VMEM((D_HEADS, LANES), F32),
                            pltpu.VMEM((N_HEADS, SUBLANES, LANES), F32),
                            pltpu.VMEM((N_HEADS, SUBLANES, LANES), F32),
                            pltpu.VMEM((N_HEADS, SUBLANES, LANES), F32),
                            pltpu.VMEM((D_HEADS, LANES), F32),
                            pltpu.VMEM((D_HEADS, nb), F32)]),
        out_shape=[jax.ShapeDtypeStruct((nb, D_HEADS), F32),
                   jax.ShapeDtypeStruct((N_HEADS, nb), F32)],
        compiler_params=_cparams(("arbitrary", "arbitrary")),
        name="fox_decode",
    )(page_table, tt, tt, tt, tt, par["bf_col"], par["u_suffix"], *([ck] * pg), *([cv] * pg), *([clf] * pg))


def _prepare_params(w_in, mu, w0, w2, a0, a2, g2, k_k, k_a, r_k, lnx_g, lnx_b, b_f,
                    w_br, w_bf, w_o, ln1_g, ln1_b, w_up, w_down, ln2_g, ln2_b):
    wt = w_in.T
    o = 0
    seg = {}
    for name, n in (("r", 512), ("k", 512), ("v", 512), ("xw", LORA_W), ("xa", LORA_A), ("xg", LORA_G),
                    ("q", 512), ("kf", 512), ("vf", 512), ("f", N_HEADS), ("gr", 1024), ("gf", 1024)):
        seg[name] = wt[o:o + n]
        o += n
    z = lambda n: jnp.zeros((n, D_MODEL), F32)
    lora = jnp.concatenate([seg["xw"], seg["xa"], seg["xg"], z(F_OFF - 288), seg["f"],
                            z(LORA_TILE - F_OFF - N_HEADS)], axis=0)
    fpad = jnp.concatenate([seg["f"], z(LANES - N_HEADS)], axis=0)
    wtok = jnp.concatenate([seg["gr"], seg["gf"], seg["r"], seg["k"], seg["v"], lora, seg["q"]], axis=0)
    wt_prompt = jnp.concatenate([seg["kf"], seg["vf"], fpad], axis=0)
    wt_sample = jnp.concatenate([seg["q"], seg["kf"], seg["vf"], fpad], axis=0)
    row = lambda x: x.reshape(1, -1).astype(F32)
    hid = jnp.arange(D_HEADS) // HEAD_DIM
    i128 = jnp.arange(LANES)
    upper_incl = (i128[:, None] <= i128[None, :])
    lower_incl = (i128[:, None] >= i128[None, :])
    ones = jnp.ones((LANES, LANES), bool)
    return dict(
        wtok=wtok.astype(BF16), wt_prompt=wt_prompt.astype(BF16), wt_sample=wt_sample.astype(BF16),
        mu=jnp.pad(row(mu), ((0, 0), (0, RW - RWKV_COLS))),
        w0=row(w0), a0=row(a0), k_k=row(k_k), k_a=row(k_a), r_k=row(r_k), lnx_g=row(lnx_g), lnx_b=row(lnx_b),
        w2p=jnp.pad(w2, ((0, LANES - LORA_W), (0, 0))).astype(BF16),
        a2p=jnp.pad(a2, ((LORA_W, 0), (0, 0))).astype(BF16),
        g2p=jnp.pad(g2, ((0, 2 * LANES - LORA_G), (0, 0))).astype(BF16),
        bones=(hid[:, None] == hid[None, :]).astype(BF16),
        tri=lower_incl.astype(BF16),
        bf_col=jnp.broadcast_to(b_f.reshape(N_HEADS, 1), (N_HEADS, LANES)).astype(F32),
        bf_row=jnp.pad(row(b_f), ((0, 0), (0, LANES - N_HEADS))),
        u_lane=jnp.concatenate([upper_incl, ones], axis=1).astype(BF16),
        l_sub=jnp.concatenate([lower_incl, ones], axis=0).astype(BF16),
        u_suffix=jnp.concatenate([i128[:, None] > i128[None, :], ones], axis=1).astype(BF16),
        w_br=w_br.astype(BF16), w_bf=w_bf.astype(BF16), w_o=w_o.astype(BF16),
        ln1_g=row(ln1_g), ln1_b=row(ln1_b), w_up=w_up.astype(BF16), w_down=w_down.astype(BF16),
        ln2_g=row(ln2_g), ln2_b=row(ln2_b))


def _unpack_shift(rows):
    return rows[:, :RWKV_COLS]


def _prompt_group(x_prompt, par):
    nb, t, _ = x_prompt.shape
    x = x_prompt.reshape(nb * t, D_MODEL)
    tok, tt = _project(x, par["wtok"], par["wt_prompt"], nb, 256)
    out_r, wkv, shift = _rwkv_prompt(tok, nb, par)
    lft, ct, ccol = _fox_gate(tok, tt, nb, par)
    o_f = _fox_attn(tok, tt, ct, ccol, nb, 256)
    h = _merge(x, tok, out_r, o_f, par, 256)
    y = _ffn(h, par, 256)
    kt = tt[:, 0:D_HEADS].reshape(nb, N_HEADS, HEAD_DIM, t)
    vt = tt[:, D_HEADS:2 * D_HEADS].reshape(nb, N_HEADS, HEAD_DIM, t)
    k_out = jnp.transpose(kt, (0, 3, 1, 2))[None]
    v_out = jnp.transpose(vt, (0, 3, 1, 2))[None]
    lf_out = jnp.transpose(lft, (0, 2, 1))[None]
    wkv_out = jnp.transpose(wkv.reshape(nb, N_PAIRS, HEAD_DIM, 2, HEAD_DIM), (0, 1, 3, 2, 4))
    wkv_out = wkv_out.reshape(1, nb, N_HEADS, HEAD_DIM, HEAD_DIM)
    return (y.reshape(nb, t, D_MODEL), k_out, v_out, lf_out, wkv_out,
            _unpack_shift(shift.reshape(nb, RW))[None])


def _sample_group(x_sample, cache_k, cache_v, cache_logf, page_table, state_wkv, state_shift, par):
    nb = x_sample.shape[0]
    x = x_sample.reshape(nb, D_MODEL)
    tok, tt = _project(x, par["wtok"], par["wt_sample"], 1, nb)
    prev = jnp.pad(state_shift, ((0, 0), (0, RW - RWKV_COLS)))
    state_t = jnp.transpose(state_wkv, (1, 2, 3, 0))[None]
    out_r, state_o = _rwkv_sample(tok, prev, state_t, par)
    ck = jnp.transpose(cache_k, (0, 2, 3, 1))[None]
    cv = jnp.transpose(cache_v, (0, 2, 3, 1))[None]
    clf = jnp.transpose(cache_logf, (0, 2, 1))[None]
    o_f, lfn = _fox_decode(tt, ck, cv, clf, page_table, par)
    h = _merge(x, tok, out_r, o_f, par, nb)
    y = _ffn(h, par, nb)
    kt = tt[0, D_HEADS:2 * D_HEADS].reshape(N_HEADS, HEAD_DIM, nb)
    vt = tt[0, 2 * D_HEADS:3 * D_HEADS].reshape(N_HEADS, HEAD_DIM, nb)
    k_out = jnp.transpose(kt, (2, 0, 1)).reshape(1, nb, 1, N_HEADS, HEAD_DIM)
    v_out = jnp.transpose(vt, (2, 0, 1)).reshape(1, nb, 1, N_HEADS, HEAD_DIM)
    lf_out = jnp.transpose(lfn, (1, 0)).reshape(1, nb, 1, N_HEADS)
    wkv_out = jnp.transpose(state_o[0], (3, 0, 1, 2))[None]
    shift_out = tok[:, COL_R:COL_R + RWKV_COLS][None]
    return y.reshape(nb, 1, D_MODEL), k_out, v_out, lf_out, wkv_out, shift_out


def kernel(x_prompt, x_sample, cache_k, cache_v, cache_logf, page_table, state_wkv, state_shift, w_in, mu, w0, w2, a0, a2, g2, k_k, k_a, r_k, lnx_g, lnx_b, b_f, w_br, w_bf, w_o, ln1_g, ln1_b, w_up, w_down, ln2_g, ln2_b):
    assert w_in.shape[0] == 1, "single-layer trunk"
    par = _prepare_params(w_in[0], mu[0], w0[0], w2[0], a0[0], a2[0], g2[0], k_k[0], k_a[0],
                          r_k[0].reshape(-1), lnx_g[0], lnx_b[0], b_f[0], w_br[0], w_bf[0], w_o[0],
                          ln1_g[0], ln1_b[0], w_up[0], w_down[0], ln2_g[0], ln2_b[0])
    yp, kp, vp, lfp, wp, sp = _prompt_group(x_prompt, par)
    ys, kd, vd, lfd, wd, sd = _sample_group(x_sample, cache_k[0], cache_v[0], cache_logf[0], page_table,
                                            state_wkv[0], state_shift[0], par)
    return (yp, ys, kp, vp, lfp, wp, sp, kd, vd, lfd, wd, sd)
```

```python
import functools

import jax
import jax.numpy as jnp
from jax import lax
from jax.experimental import pallas as pl
from jax.experimental.pallas import tpu as pltpu

F32 = jnp.float32
BF16 = jnp.bfloat16

D_MODEL = 1024
HEAD_DIM = 64
N_HEADS = 8
D_HEADS = N_HEADS * HEAD_DIM
N_PAIRS = N_HEADS // 2
LANES = 128
SUBLANES = 8
LORA_W, LORA_A, LORA_G = 64, 64, 160
RWKV_COLS = 3 * D_HEADS + LORA_W + LORA_A + LORA_G
D_FF = 2816
LN_EPS = 1e-5
GN_EPS = 64e-5
NEG = -1e30
SCALE = HEAD_DIM ** -0.5
ALPHA = 2.0 ** 0.25
CHUNK = 128
VMEM_LIMIT = 56 * 1024 * 1024

LORA_TILE = 512
F_OFF = 384
COL_GR, COL_GF, COL_R, COL_Q = 0, 1024, 2048, 4096
N_TOK = 4608
RW = 2048


def _dot(a, b):
    return jnp.dot(a, b, preferred_element_type=F32)


def _dot_nt(a, b):
    return lax.dot_general(a, b, (((1,), (1,)), ((), ())), preferred_element_type=F32)


def _dot_tn(a, b):
    return lax.dot_general(a, b, (((0,), (0,)), ((), ())), preferred_element_type=F32)


def _split3(x):
    hi = x.astype(BF16)
    r1 = x - hi.astype(F32)
    mid = r1.astype(BF16)
    lo = (r1 - mid.astype(F32)).astype(BF16)
    return hi, mid, lo


def _dot_x3(x, m):
    hi, mid, lo = _split3(x)
    return _dot(hi, m) + _dot(mid, m) + _dot(lo, m)


def _dot_3x(m, x):
    hi, mid, lo = _split3(x)
    return _dot(m, hi) + _dot(m, mid) + _dot(m, lo)


def _sigmoid(z):
    return 1.0 / (1.0 + jnp.exp(-z))


def _softplus(z):
    return jnp.maximum(z, 0.0) + jnp.log1p(jnp.exp(-jnp.abs(z)))


def _log_sigmoid(z):
    return -_softplus(-z)


def _cparams(sem):
    return pltpu.CompilerParams(dimension_semantics=sem, vmem_limit_bytes=VMEM_LIMIT)


def _const_spec(shape):
    nd = len(shape)
    return pl.BlockSpec(shape, lambda *_: (0,) * nd, pipeline_mode=pl.Buffered(1))


def _proj_kernel(x_ref, wtok_ref, wt_ref, tok_ref, t_ref, *, tn):
    xb = x_ref[...].astype(BF16)
    for j in range(wtok_ref.shape[0] // tn):
        tok_ref[:, j * tn:(j + 1) * tn] = _dot_nt(xb, wtok_ref[j * tn:(j + 1) * tn, :])
    t_ref[0] = _dot_nt(wt_ref[...], xb)


def _project(x, wtok, wt, nb, tm):
    m = x.shape[0]
    t = m // nb
    nt = t // tm
    return pl.pallas_call(
        functools.partial(_proj_kernel, tn=512),
        grid=(m // tm,),
        in_specs=[pl.BlockSpec((tm, D_MODEL), lambda i: (i, 0)),
                  _const_spec(wtok.shape), _const_spec(wt.shape)],
        out_specs=[pl.BlockSpec((tm, wtok.shape[0]), lambda i: (i, 0)),
                   pl.BlockSpec((1, wt.shape[0], tm), lambda i: (i // nt, 0, i % nt))],
        out_shape=[jax.ShapeDtypeStruct((m, wtok.shape[0]), F32),
                   jax.ShapeDtypeStruct((nb, wt.shape[0], t), F32)],
        compiler_params=_cparams(("arbitrary",)),
        name="proj",
    )(x, wtok, wt)


def _rwkv_prep(p, prev, mu, w0, a0, k_k, k_a, w2p, a2p, g2p, bones):
    xx = p + (prev - p) * mu
    r = xx[:, 0:512]
    k = xx[:, 512:1024]
    v = xx[:, 1024:1536]
    t0 = xx[:, 1536:1664]
    t12 = xx[:, 1664:1920]
    w = -_softplus(-(w0 + _dot(jnp.tanh(t0).astype(BF16), w2p))) - 0.5
    log_w = -jnp.exp(w)
    a = _sigmoid(a0 + _dot(t0.astype(BF16), a2p))
    g = _dot(_sigmoid(t12).astype(BF16), g2p)
    kk = k * k_k
    ss = _dot_x3(kk * kk, bones)
    kk = kk * lax.rsqrt(jnp.maximum(ss, 1e-24))
    k2 = k * (1.0 + (a - 1.0) * k_a)
    return r, k2, v, kk, a, log_w, g


def _rwkv_post(y, r, k2, v, g, r_k, lnx_g, lnx_b, bones):
    inv = 1.0 / HEAD_DIM
    mean = _dot_x3(y, bones) * inv
    d = y - mean
    var = _dot_x3(d * d, bones) * inv
    yn = d * lax.rsqrt(var + GN_EPS) * lnx_g + lnx_b
    bonus = _dot_x3(r * k2 * r_k, bones) * v
    return (yn + bonus) * g


def _rwkv_chunk_kernel(p_ref, mu_ref, w0_ref, a0_ref, kk_ref, ka_ref, rk_ref, lng_ref, lnb_ref,
                       w2_ref, a2_ref, g2_ref, bones_ref, tri_ref,
                       out_ref, wkv_ref, shift_ref,
                       prev_s, wl_s, st_s, at_s, rt_s, ar_s, rr_s, br_s, kr_s, v_s, be_s, ke_s, y_s):
    c = pl.program_id(1)
    L = CHUNK

    @pl.when(c == 0)
    def _():
        prev_s[...] = jnp.zeros_like(prev_s)
        st_s[...] = jnp.zeros_like(st_s)

    p = p_ref[...]
    row = lax.broadcasted_iota(jnp.int32, (L, RW), 0)
    prev = jnp.where(row == 0, prev_s[...], pltpu.roll(p, 1, axis=0))
    last = p_ref[L - 1:L, :]
    prev_s[...] = last
    bones = bones_ref[...]
    r, k2, v, kk, a, lw, g = _rwkv_prep(p, prev, mu_ref[...], w0_ref[...], a0_ref[...], kk_ref[...],
                                        ka_ref[...], w2_ref[...], a2_ref[...], g2_ref[...], bones)
    b = kk * a
    gam = _dot_3x(tri_ref[...], lw)
    y_s[...] = gam
    gmid = y_s[L // 2 - 1:L // 2, :]
    glast = y_s[L - 1:L, :]
    e_mid_inv = jnp.exp(-gmid)
    a_true = -kk * jnp.exp(gam - lw)
    r_true = r * jnp.exp(gam)
    e_mg = jnp.exp(gmid - gam)
    e_end = jnp.exp(glast - gam)
    wl_s[...] = jnp.exp(glast)
    at_s[...] = a_true
    rt_s[...] = r_true
    ar_s[...] = a_true * e_mid_inv
    rr_s[...] = r_true * e_mid_inv
    br_s[...] = b * e_mg
    kr_s[...] = k2 * e_mg
    v_s[...] = v
    be_s[...] = b * e_end
    ke_s[...] = k2 * e_end

    lane = lax.broadcasted_iota(jnp.int32, (L, LANES), 1)
    rowl = lax.broadcasted_iota(jnp.int32, (L, LANES), 0)
    lo = lane < HEAD_DIM
    strict = rowl > lane
    incl = rowl >= lane
    lo64 = lax.broadcasted_iota(jnp.int32, (HEAD_DIM, LANES), 1) < HEAD_DIM
    n_dbl = 7

    pairs = []
    for pp in range(N_PAIRS):
        sl = slice(pp * LANES, (pp + 1) * LANES)
        a_t = at_s[:, sl]
        r_t = rt_s[:, sl]
        pairs.append(dict(
            sl=sl, a_r=ar_s[:, sl], r_r=rr_s[:, sl], vb=v_s[:, sl].astype(BF16),
            rhs=jnp.concatenate([br_s[:, sl], kr_s[:, sl]], axis=0).astype(BF16),
            be=be_s[:, sl].astype(BF16), ke=ke_s[:, sl].astype(BF16),
            a_sw=pltpu.roll(a_t, HEAD_DIM, axis=1), r_sw=pltpu.roll(r_t, HEAD_DIM, axis=1)))
    heads = [(pp, hh) for pp in range(N_PAIRS) for hh in range(2)]
    gms = []
    for pp, hh in heads:
        d = pairs[pp]
        msk = lo if hh == 0 else jnp.logical_not(lo)
        lhs = jnp.concatenate([jnp.where(msk, d["a_r"], 0.0), jnp.where(msk, d["r_r"], 0.0)],
                              axis=0).astype(BF16)
        gms.append(_dot_nt(lhs, d["rhs"]))
    pms = [jnp.where(strict, gm[:L, :L], 0.0).astype(BF16) for gm in gms]
    aaks = [jnp.where(strict, gm[:L, L:], 0.0).astype(BF16) for gm in gms]
    arbs = [jnp.where(incl, gm[L:, :L], 0.0).astype(BF16) for gm in gms]
    arks = [jnp.where(incl, gm[L:, L:], 0.0).astype(BF16) for gm in gms]
    zs = []
    for i, (pp, hh) in enumerate(heads):
        d = pairs[pp]
        av = _dot(aaks[i], d["vb"])
        zs.append(jnp.where(lo, av, d["a_sw"]) if hh == 0 else jnp.where(lo, d["a_sw"], av))
    for it in range(n_dbl):
        zs = [z + _dot(pm, z.astype(BF16)) for z, pm in zip(zs, pms)]
        if it < n_dbl - 1:
            pms = [_dot(pm, pm).astype(BF16) for pm in pms]
    zbs = [z.astype(BF16) for z in zs]
    rys = []
    qqs = []
    for i, (pp, hh) in enumerate(heads):
        d = pairs[pp]
        avk = _dot(arks[i], d["vb"])
        tail = jnp.where(lo, avk, d["r_sw"]) if hh == 0 else jnp.where(lo, d["r_sw"], avk)
        rys.append(_dot(arbs[i], zbs[i]) + tail)
        qqs.append(_dot_tn(zbs[i], d["be"]))
    for pp in range(N_PAIRS):
        d = pairs[pp]
        sl = d["sl"]
        ry = rys[2 * pp:2 * pp + 2]
        qq = qqs[2 * pp:2 * pp + 2]
        vk = _dot_tn(d["vb"], d["ke"])
        rcomb = jnp.where(lo, ry[1], ry[0])
        y0 = jnp.where(lo, ry[0], ry[1])
        wl_p = wl_s[:, sl]
        mpair = jnp.concatenate([jnp.where(lo64, qq[0][HEAD_DIM:], 0.0),
                                 jnp.where(lo64, 0.0, qq[1][:HEAD_DIM])], axis=0)
        mpair = mpair + jnp.where(rowl == lane, wl_p, 0.0)
        npair = jnp.where(lo64, qq[0][:HEAD_DIM] + vk[:HEAD_DIM], qq[1][HEAD_DIM:] + vk[HEAD_DIM:])
        sp = st_s[pp]
        ssw = pltpu.roll(sp, HEAD_DIM, axis=1)
        santi = jnp.concatenate([jnp.where(lo64, 0.0, ssw), jnp.where(lo64, ssw, 0.0)], axis=0)
        y_s[:, sl] = _dot_nt(rcomb.astype(BF16), santi.astype(BF16)) + y0
        st_s[pp] = _dot(sp.astype(BF16), mpair.astype(BF16)) + npair

    out_ref[...] = _rwkv_post(y_s[...], r, k2, v, g, rk_ref[...], lng_ref[...], lnb_ref[...], bones)

    @pl.when(c == pl.num_programs(1) - 1)
    def _():
        wkv_ref[0] = st_s[...]
        shift_ref[0] = last


def _rwkv_prompt(tok, nb, par):
    m = tok.shape[0]
    nc = m // nb // CHUNK
    vec = pl.BlockSpec((1, D_HEADS), lambda b, c: (0, 0))
    return pl.pallas_call(
        _rwkv_chunk_kernel,
        grid=(nb, nc),
        in_specs=[pl.BlockSpec((CHUNK, RW), lambda b, c: (b * nc + c, COL_R // RW)),
                  pl.BlockSpec((1, RW), lambda b, c: (0, 0)),
                  vec, vec, vec, vec, vec, vec, vec,
                  pl.BlockSpec((LANES, D_HEADS), lambda b, c: (0, 0)),
                  pl.BlockSpec((LANES, D_HEADS), lambda b, c: (0, 0)),
                  pl.BlockSpec((2 * LANES, D_HEADS), lambda b, c: (0, 0)),
                  pl.BlockSpec((D_HEADS, D_HEADS), lambda b, c: (0, 0)),
                  pl.BlockSpec((CHUNK, CHUNK), lambda b, c: (0, 0))],
        out_specs=[pl.BlockSpec((CHUNK, D_HEADS), lambda b, c: (b * nc + c, 0)),
                   pl.BlockSpec((1, N_PAIRS, HEAD_DIM, LANES), lambda b, c: (b, 0, 0, 0)),
                   pl.BlockSpec((1, 1, RW), lambda b, c: (b, 0, 0))],
        out_shape=[jax.ShapeDtypeStruct((m, D_HEADS), F32),
                   jax.ShapeDtypeStruct((nb, N_PAIRS, HEAD_DIM, LANES), F32),
                   jax.ShapeDtypeStruct((nb, 1, RW), F32)],
        scratch_shapes=[pltpu.VMEM((1, RW), F32), pltpu.VMEM((1, D_HEADS), F32),
                        pltpu.VMEM((N_PAIRS, HEAD_DIM, LANES), F32)]
                       + [pltpu.VMEM((CHUNK, D_HEADS), F32)] * 10,
        compiler_params=_cparams(("arbitrary", "arbitrary")),
        name="rwkv_chunk",
    )(tok, par["mu"], par["w0"], par["a0"], par["k_k"], par["k_a"], par["r_k"], par["lnx_g"],
      par["lnx_b"], par["w2p"], par["a2p"], par["g2p"], par["bones"], par["tri"])


def _fox_gate_kernel(ft_ref, ftok_ref, bfc_ref, bfr_ref, ul_ref, ll_ref, lft_ref, ct_ref, ccol_ref):
    t = ft_ref.shape[2]
    lft = _log_sigmoid(ft_ref[0][:N_HEADS, :] + bfc_ref[:, 0:1])
    lft_ref[0] = lft
    carry = jnp.zeros((N_HEADS, LANES), F32)
    for blk in range(t // LANES):
        cs = _dot_x3(lft[:, blk * LANES:(blk + 1) * LANES], ul_ref[...])
        ct_ref[0, :, blk * LANES:(blk + 1) * LANES] = cs[:, :LANES] + carry
        carry = carry + cs[:, LANES:]
    carry_r = jnp.zeros((LANES, LANES), F32)
    for blk in range(t // LANES):
        lf = _log_sigmoid(ftok_ref[blk * LANES:(blk + 1) * LANES, :] + bfr_ref[...])
        cs = _dot_3x(ll_ref[...], lf)
        ccol_ref[blk * LANES:(blk + 1) * LANES, :] = cs[:LANES] + carry_r
        carry_r = carry_r + cs[LANES:]


def _fox_gate(tok, tt, nb, par):
    m = tok.shape[0]
    t = m // nb
    fcol = (COL_R + 3 * D_HEADS + F_OFF) // LANES
    frow = (2 * D_HEADS) // LANES
    return pl.pallas_call(
        _fox_gate_kernel,
        grid=(nb,),
        in_specs=[pl.BlockSpec((1, LANES, t), lambda b: (b, frow, 0)),
                  pl.BlockSpec((t, LANES), lambda b: (b, fcol)),
                  pl.BlockSpec((N_HEADS, LANES), lambda b: (0, 0)),
                  pl.BlockSpec((1, LANES), lambda b: (0, 0)),
                  pl.BlockSpec((LANES, 2 * LANES), lambda b: (0, 0)),
                  pl.BlockSpec((2 * LANES, LANES), lambda b: (0, 0))],
        out_specs=[pl.BlockSpec((1, N_HEADS, t), lambda b: (b, 0, 0)),
                   pl.BlockSpec((1, N_HEADS, t), lambda b: (b, 0, 0)),
                   pl.BlockSpec((t, LANES), lambda b: (b, 0))],
        out_shape=[jax.ShapeDtypeStruct((nb, N_HEADS, t), F32),
                   jax.ShapeDtypeStruct((nb, N_HEADS, t), F32),
                   jax.ShapeDtypeStruct((m, LANES), F32)],
        compiler_params=_cparams(("arbitrary",)),
        name="fox_gate",
    )(tt, tok, par["bf_col"], par["bf_row"], par["u_lane"], par["l_sub"])


def _fox_attn_kernel(q_ref, kt_ref, vt_ref, ct_ref, ccol_ref, o_ref,
                     kb_s, vb_s, qh_s, cq_s, m_s, l_s, acc_s, *, tq):
    qi = pl.program_id(1)

    @pl.when(qi == 0)
    def _():
        kb_s[...] = kt_ref[0].astype(BF16)
        vb_s[...] = vt_ref[0].astype(BF16)

    lane = lax.broadcasted_iota(jnp.int32, (tq, LANES), 1)
    lo = lane < HEAD_DIM
    rowq = lax.broadcasted_iota(jnp.int32, (tq, tq), 0)
    colq = lax.broadcasted_iota(jnp.int32, (tq, tq), 1)
    causal = colq <= rowq
    ccol = ccol_ref[...]

    for pp in range(N_PAIRS):
        q = q_ref[:, pp * LANES:(pp + 1) * LANES] * SCALE
        qh_s[2 * pp] = jnp.where(lo, q, 0.0).astype(BF16)
        qh_s[2 * pp + 1] = jnp.where(lo, 0.0, q).astype(BF16)
    for h in range(N_HEADS):
        cq_s[h] = ccol[:, h:h + 1]
    m_s[...] = jnp.full_like(m_s, NEG)
    l_s[...] = jnp.zeros_like(l_s)
    acc_s[...] = jnp.zeros_like(acc_s)

    def step(j, masked):
        off = pl.multiple_of(j * tq, tq)
        ss = []
        for h in range(N_HEADS):
            rs = slice((h // 2) * LANES, (h // 2 + 1) * LANES)
            s = _dot(qh_s[h], kb_s[rs, pl.ds(off, tq)]) + cq_s[h] - ct_ref[0, h:h + 1, pl.ds(off, tq)]
            ss.append(jnp.where(causal, s, NEG) if masked else s)
        m_old = [m_s[h] for h in range(N_HEADS)]
        m_new = [jnp.maximum(m_old[h], jnp.max(ss[h], axis=1, keepdims=True)) for h in range(N_HEADS)]
        alpha = [jnp.exp(m_old[h] - m_new[h]) for h in range(N_HEADS)]
        pes = [jnp.exp(ss[h] - m_new[h]) for h in range(N_HEADS)]
        for h in range(N_HEADS):
            m_s[h] = m_new[h]
            l_s[h] = l_s[h] * alpha[h] + jnp.sum(pes[h], axis=1, keepdims=True)
        for h in range(N_HEADS):
            rs = slice((h // 2) * LANES, (h // 2 + 1) * LANES)
            acc_s[h] = acc_s[h] * alpha[h] + _dot_nt(pes[h].astype(BF16), vb_s[rs, pl.ds(off, tq)])

    def body(j, carry):
        step(j, False)
        return carry

    lax.fori_loop(0, qi, body, 0)
    step(qi, True)
    for pp in range(N_PAIRS):
        oa = acc_s[2 * pp] / l_s[2 * pp]
        ob = acc_s[2 * pp + 1] / l_s[2 * pp + 1]
        o_ref[:, pp * LANES:(pp + 1) * LANES] = jnp.where(lo, oa, ob)


def _fox_attn(tok, tt, ct, ccol, nb, tq):
    m = tok.shape[0]
    t = m // nb
    nq = t // tq
    return pl.pallas_call(
        functools.partial(_fox_attn_kernel, tq=tq),
        grid=(nb, nq),
        in_specs=[pl.BlockSpec((tq, D_HEADS), lambda b, i: (b * nq + i, COL_Q // D_HEADS)),
                  pl.BlockSpec((1, D_HEADS, t), lambda b, i: (b, 0, 0)),
                  pl.BlockSpec((1, D_HEADS, t), lambda b, i: (b, 1, 0)),
                  pl.BlockSpec((1, N_HEADS, t), lambda b, i: (b, 0, 0)),
                  pl.BlockSpec((tq, LANES), lambda b, i: (b * nq + i, 0))],
        out_specs=pl.BlockSpec((tq, D_HEADS), lambda b, i: (b * nq + i, 0)),
        out_shape=jax.ShapeDtypeStruct((m, D_HEADS), F32),
        scratch_shapes=[pltpu.VMEM((D_HEADS, t), BF16), pltpu.VMEM((D_HEADS, t), BF16),
                        pltpu.VMEM((N_HEADS, tq, LANES), BF16), pltpu.VMEM((N_HEADS, tq, 1), F32),
                        pltpu.VMEM((N_HEADS, tq, 1), F32), pltpu.VMEM((N_HEADS, tq, 1), F32),
                        pltpu.VMEM((N_HEADS, tq, LANES), F32)],
        compiler_params=_cparams(("arbitrary", "arbitrary")),
        name="fox_attn",
    )(tok, tt, tt, ct, ccol)


def _layer_norm(z, g, b):
    mu = jnp.mean(z, axis=-1, keepdims=True)
    d = z - mu
    var = jnp.mean(d * d, axis=-1, keepdims=True)
    return d * lax.rsqrt(var + LN_EPS) * g + b


def _merge_kernel(x_ref, gr_ref, gf_ref, r_ref, f_ref, wbr_ref, wbf_ref, wo_ref, g_ref, b_ref, h_ref):
    a = _dot(r_ref[...].astype(BF16), wbr_ref[...])
    b = _dot(f_ref[...].astype(BF16), wbf_ref[...])
    mixed = _sigmoid(gr_ref[...]) * a + _sigmoid(gf_ref[...]) * b
    z = ALPHA * x_ref[...] + _dot(mixed.astype(BF16), wo_ref[...])
    h_ref[...] = _layer_norm(z, g_ref[...], b_ref[...])


def _merge(x, tok, out_r, o_f, par, tm):
    m = x.shape[0]
    row = lambda c: pl.BlockSpec((tm, c), lambda i: (i, 0))
    return pl.pallas_call(
        _merge_kernel,
        grid=(m // tm,),
        in_specs=[row(D_MODEL),
                  pl.BlockSpec((tm, D_MODEL), lambda i: (i, COL_GR // D_MODEL)),
                  pl.BlockSpec((tm, D_MODEL), lambda i: (i, COL_GF // D_MODEL)),
                  row(D_HEADS), row(D_HEADS),
                  _const_spec((D_HEADS, D_MODEL)), _const_spec((D_HEADS, D_MODEL)),
                  _const_spec((D_MODEL, D_MODEL)), _const_spec((1, D_MODEL)), _const_spec((1, D_MODEL))],
        out_specs=row(D_MODEL),
        out_shape=jax.ShapeDtypeStruct((m, D_MODEL), F32),
        compiler_params=_cparams(("arbitrary",)),
        name="merge",
    )(x, tok, tok, out_r, o_f, par["w_br"], par["w_bf"], par["w_o"], par["ln1_g"], par["ln1_b"])


FF_CHUNK = D_FF // 2


def _ffn_kernel(h_ref, wup_ref, wdn_ref, g_ref, b_ref, y_ref):
    h = h_ref[...]
    hb = h.astype(BF16)
    acc = ALPHA * h
    for ci in range(D_FF // FF_CHUNK):
        cs = slice(ci * FF_CHUNK, (ci + 1) * FF_CHUNK)
        ug = _dot(hb, wup_ref[:, cs])
        uv = _dot(hb, wup_ref[:, D_FF + ci * FF_CHUNK:D_FF + (ci + 1) * FF_CHUNK])
        act = (ug * _sigmoid(ug) * uv).astype(BF16)
        acc = acc + _dot(act, wdn_ref[cs, :])
    y_ref[...] = _layer_norm(acc, g_ref[...], b_ref[...])


def _ffn(h, par, tm):
    m = h.shape[0]
    return pl.pallas_call(
        _ffn_kernel,
        grid=(m // tm,),
        in_specs=[pl.BlockSpec((tm, D_MODEL), lambda i: (i, 0)),
                  _const_spec((D_MODEL, 2 * D_FF)), _const_spec((D_FF, D_MODEL)),
                  _const_spec((1, D_MODEL)), _const_spec((1, D_MODEL))],
        out_specs=pl.BlockSpec((tm, D_MODEL), lambda i: (i, 0)),
        out_shape=jax.ShapeDtypeStruct((m, D_MODEL), F32),
        compiler_params=_cparams(("arbitrary",)),
        name="ffn",
    )(h, par["w_up"], par["w_down"], par["ln2_g"], par["ln2_b"])


def _rwkv_step_kernel(p_ref, prev_ref, mu_ref, w0_ref, a0_ref, kk_ref, ka_ref, rk_ref, lng_ref, lnb_ref,
                      w2_ref, a2_ref, g2_ref, bones_ref, s_ref,
                      out_ref, so_ref,
                      r_s, k_s, v_s, g_s, rt_s, kt_s, vt_s, nk_s, bt_s, dt_s, yt_s):
    h = pl.program_id(0)

    @pl.when(h == 0)
    def _():
        r, k2, v, kk, a, lw, g = _rwkv_prep(p_ref[...], prev_ref[...], mu_ref[...], w0_ref[...],
                                            a0_ref[...], kk_ref[...], ka_ref[...], w2_ref[...],
                                            a2_ref[...], g2_ref[...], bones_ref[...])
        r_s[...] = r
        k_s[...] = k2
        v_s[...] = v
        g_s[...] = g
        rt_s[...] = r.T
        kt_s[...] = k2.T
        vt_s[...] = v.T
        nk_s[...] = (-kk).T
        bt_s[...] = (kk * a).T
        dt_s[...] = jnp.exp(lw).T

    hs = pl.ds(pl.multiple_of(h * HEAD_DIM, HEAD_DIM), HEAD_DIM)
    nkk = nk_s[hs, :]
    dec = dt_s[hs, :]
    bb = bt_s[hs, :]
    kk2 = kt_s[hs, :]
    rr = rt_s[hs, :]

    def body(vi, _):
        s = s_ref[0, 0, vi]
        sa = jnp.sum(s * nkk, axis=0, keepdims=True)
        vrow = vt_s[pl.ds(h * HEAD_DIM + vi, 1), :]
        s2 = s * dec + sa * bb + vrow * kk2
        so_ref[0, 0, vi] = s2
        yt_s[pl.ds(h * HEAD_DIM + vi, 1), :] = jnp.sum(s2 * rr, axis=0, keepdims=True)
        return 0

    lax.fori_loop(0, HEAD_DIM, body, 0)

    @pl.when(h == N_HEADS - 1)
    def _():
        out_ref[...] = _rwkv_post(yt_s[...].T, r_s[...], k_s[...], v_s[...], g_s[...], rk_ref[...],
                                  lng_ref[...], lnb_ref[...], bones_ref[...])


def _rwkv_sample(tok, prev, state_t, par):
    nb = tok.shape[0]
    vec = pl.BlockSpec((1, D_HEADS), lambda h: (0, 0))
    sspec = pl.BlockSpec((1, 1, HEAD_DIM, HEAD_DIM, nb), lambda h: (0, h, 0, 0, 0))
    return pl.pallas_call(
        _rwkv_step_kernel,
        grid=(N_HEADS,),
        in_specs=[pl.BlockSpec((nb, RW), lambda h: (0, COL_R // RW)),
                  pl.BlockSpec((nb, RW), lambda h: (0, 0)),
                  pl.BlockSpec((1, RW), lambda h: (0, 0)),
                  vec, vec, vec, vec, vec, vec, vec,
                  pl.BlockSpec((LANES, D_HEADS), lambda h: (0, 0)),
                  pl.BlockSpec((LANES, D_HEADS), lambda h: (0, 0)),
                  pl.BlockSpec((2 * LANES, D_HEADS), lambda h: (0, 0)),
                  pl.BlockSpec((D_HEADS, D_HEADS), lambda h: (0, 0)),
                  sspec],
        out_specs=[pl.BlockSpec((nb, D_HEADS), lambda h: (0, 0)), sspec],
        out_shape=[jax.ShapeDtypeStruct((nb, D_HEADS), F32),
                   jax.ShapeDtypeStruct(state_t.shape, F32)],
        scratch_shapes=[pltpu.VMEM((nb, D_HEADS), F32)] * 4 + [pltpu.VMEM((D_HEADS, nb), F32)] * 7,
        compiler_params=_cparams(("arbitrary",)),
        name="rwkv_step",
    )(tok, prev, par["mu"], par["w0"], par["a0"], par["k_k"], par["k_a"], par["r_k"], par["lnx_g"],
      par["lnx_b"], par["w2p"], par["a2p"], par["g2p"], par["bones"], state_t)


PAGE_GROUP = 8
DECODE_SLOTS = 3


def _allreduce_sublanes(x):
    x = x + pltpu.roll(x, 4, axis=0)
    x = x + pltpu.roll(x, 2, axis=0)
    return x + pltpu.roll(x, 1, axis=0)


def _fold8(x):
    return jnp.sum(x.reshape(HEAD_DIM // SUBLANES, SUBLANES, LANES), axis=0)


def _tile8(x):
    return jnp.broadcast_to(x[None], (HEAD_DIM // SUBLANES, SUBLANES, LANES)).reshape(HEAD_DIM, LANES)


def _fox_decode_kernel(pt_ref, qt_ref, kn_ref, vn_ref, ft_ref, bfc_ref, us_ref, ck_ref, cv_ref, clf_ref,
                       o_ref, lfn_ref, kbuf, vbuf, lfbuf, sem_kv, sem_lf,
                       qb_s, m_s, l_s, c_s, acc_s, ot_s, *, n_groups, n_steps):
    pg = PAGE_GROUP
    b = pl.program_id(0)
    j = pl.program_id(1)
    nb = pl.num_programs(0)
    g = b * n_groups + j

    def group_copies(gi):
        if isinstance(gi, int):
            bi, ji, slot = gi // n_groups, gi % n_groups, gi % DECODE_SLOTS
        else:
            bi, ji, slot = lax.div(gi, n_groups), lax.rem(gi, n_groups), lax.rem(gi, DECODE_SLOTS)
        out = []
        for i in range(pg):
            page = pt_ref[bi, (n_groups - 1 - ji) * pg + i]
            out.append((pltpu.make_async_copy(ck_ref.at[0, page], kbuf.at[slot, i], sem_kv.at[slot]), i % 2))
            out.append((pltpu.make_async_copy(cv_ref.at[0, page], vbuf.at[slot, i], sem_kv.at[slot]),
                        (i + 1) % 2))
            out.append((pltpu.make_async_copy(clf_ref.at[0, page], lfbuf.at[slot, i], sem_lf.at[slot]), 0))
        return out

    @pl.when(g == 0)
    def _():
        for gi in range(DECODE_SLOTS - 1):
            for cp, prio in group_copies(gi):
                cp.start(priority=prio)

    @pl.when(g + (DECODE_SLOTS - 1) < n_steps)
    def _():
        for cp, prio in group_copies(g + (DECODE_SLOTS - 1)):
            cp.start(priority=prio)

    for cp, _ in group_copies(g):
        cp.wait()
    slot = lax.rem(g, DECODE_SLOTS)
    rowsel = lax.broadcasted_iota(jnp.int32, (LANES, LANES), 0)
    lanes8 = lax.broadcasted_iota(jnp.int32, (SUBLANES, LANES), 1)
    lfn_all = _log_sigmoid(ft_ref[0][:N_HEADS, :] + bfc_ref[:, 0:1])

    @pl.when(j == 0)
    def _():
        sel = (rowsel == b).astype(BF16)
        qb_s[...] = _dot_x3(qt_ref[0], sel) * SCALE
        cq = _dot_x3(lfn_all, sel)
        for hh in range(N_HEADS):
            c_s[hh] = jnp.broadcast_to(cq[hh:hh + 1, :], (SUBLANES, LANES))
        m_s[...] = jnp.full_like(m_s, NEG)
        l_s[...] = jnp.zeros_like(l_s)
        acc_s[...] = jnp.zeros_like(acc_s)

    @pl.when(jnp.logical_and(b == 0, j == 0))
    def _():
        lfn_ref[...] = lfn_all
        ot_s[...] = jnp.zeros_like(ot_s)

    su_all = _dot_x3(lfbuf[slot].reshape(pg * N_HEADS, LANES), us_ref[...])
    sus = [su_all[i * N_HEADS:(i + 1) * N_HEADS] for i in range(pg)]
    for hh in range(N_HEADS):
        hs = slice(hh * HEAD_DIM, (hh + 1) * HEAD_DIM)
        qh = qb_s[hs, :]
        run = c_s[hh]
        scores = [None] * pg
        for i in range(pg - 1, -1, -1):
            bias = jnp.broadcast_to(sus[i][hh:hh + 1, :LANES], (SUBLANES, LANES)) + run
            scores[i] = _allreduce_sublanes(_fold8(kbuf[slot, i, hh] * qh)) + bias
            run = run + jnp.broadcast_to(sus[i][hh:hh + 1, LANES:], (SUBLANES, LANES))
        c_s[hh] = run
        m_old = m_s[hh]
        m_new = m_old
        for s in scores:
            m_new = jnp.maximum(m_new, s)
        alpha = jnp.exp(m_old - m_new)
        pes = [jnp.exp(s - m_new) for s in scores]
        m_s[hh] = m_new
        l_new = l_s[hh] * alpha
        acc = acc_s[hs, :] * _tile8(alpha)
        for i in range(pg):
            l_new = l_new + pes[i]
            acc = acc + vbuf[slot, i, hh] * _tile8(pes[i])
        l_s[hh] = l_new
        acc_s[hs, :] = acc

    @pl.when(j == n_groups - 1)
    def _():
        sel = (rowsel == b).astype(BF16)
        knb = _dot_x3(kn_ref[0], sel)
        vnb = _dot_x3(vn_ref[0], sel)
        for hh in range(N_HEADS):
            hs = slice(hh * HEAD_DIM, (hh + 1) * HEAD_DIM)
            m = m_s[hh]
            mx = jnp.max(m, axis=1, keepdims=True)
            wgt = jnp.exp(m - mx)
            l_tot = jnp.sum(l_s[hh] * wgt, axis=1, keepdims=True)
            o_col = jnp.sum(acc_s[hs, :] * _tile8(wgt), axis=1, keepdims=True)
            s_new = _allreduce_sublanes(_fold8(knb[hs, :] * qb_s[hs, :]))
            m2 = jnp.maximum(mx, s_new)
            e1 = jnp.exp(mx - m2)
            e2 = jnp.exp(s_new - m2)
            inv = 1.0 / (l_tot * e1 + e2)
            o_h = (o_col * _tile8(e1) + vnb[hs, :] * _tile8(e2)) * _tile8(inv)
            lane64 = lax.broadcasted_iota(jnp.int32, (HEAD_DIM, LANES), 1)
            ot_s[hs, :] = jnp.where(lane64 == b, o_h, ot_s[hs, :])

    @pl.when(jnp.logical_and(b == nb - 1, j == n_groups - 1))
    def _():
        o_ref[...] = ot_s[...].T


def _fox_decode(tt, ck, cv, clf, page_table, par):
    nb = tt.shape[2]
    n_pages = page_table.shape[1]
    pg = PAGE_GROUP
    ng = n_pages // pg

    assert nb * ng >= DECODE_SLOTS - 1
    trow = lambda r, n: pl.BlockSpec((1, n, nb), lambda b, j, pt: (0, r, 0))
    hbm = pl.BlockSpec(memory_space=pl.ANY)
    in_specs = [trow(0, D_HEADS), trow(1, D_HEADS), trow(2, D_HEADS),
                pl.BlockSpec((1, LANES, nb), lambda b, j, pt: (0, 3 * D_HEADS // LANES, 0)),
                pl.BlockSpec((N_HEADS, LANES), lambda b, j, pt: (0, 0)),
                pl.BlockSpec((LANES, 2 * LANES), lambda b, j, pt: (0, 0)),
                hbm, hbm, hbm]
    return pl.pallas_call(
        functools.partial(_fox_decode_kernel, n_groups=ng, n_steps=nb * ng),
        grid_spec=pltpu.PrefetchScalarGridSpec(
            num_scalar_prefetch=1,
            grid=(nb, ng),
            in_specs=in_specs,
            out_specs=[pl.BlockSpec((nb, D_HEADS), lambda b, j, pt: (0, 0)),
                       pl.BlockSpec((N_HEADS, nb), lambda b, j, pt: (0, 0))],
            scratch_shapes=[pltpu.VMEM((DECODE_SLOTS, pg, N_HEADS, HEAD_DIM, LANES), F32),
                            pltpu.VMEM((DECODE_SLOTS, pg, N_HEADS, HEAD_DIM, LANES), F32),
                            pltpu.VMEM((DECODE_SLOTS, pg, N_HEADS, LANES), F32),
                            pltpu.SemaphoreType.DMA((DECODE_SLOTS,)),
                            pltpu.SemaphoreType.DMA((DECODE_SLOTS,)),
                            pltpu.VMEM((D_HEADS, LANES), F32),
                            pltpu.VMEM((N_HEADS, SUBLANES, LANES), F32),
                            pltpu.VMEM((N_HEADS, SUBLANES, LANES), F32),
                            pltpu.VMEM((N_HEADS, SUBLANES, LANES), F32),
                            pltpu.VMEM((D_HEADS, LANES), F32),
                            pltpu.VMEM((D_HEADS, nb), F32)]),
        out_shape=[jax.ShapeDtypeStruct((nb, D_HEADS), F32),
                   jax.ShapeDtypeStruct((N_HEADS, nb), F32)],
        compiler_params=_cparams(("arbitrary", "arbitrary")),
        name="fox_decode",
    )(page_table, tt, tt, tt, tt, par["bf_col"], par["u_suffix"], ck, cv, clf)


def _prepare_params(w_in, mu, w0, w2, a0, a2, g2, k_k, k_a, r_k, lnx_g, lnx_b, b_f,
                    w_br, w_bf, w_o, ln1_g, ln1_b, w_up, w_down, ln2_g, ln2_b):
    wt = w_in.T
    o = 0
    seg = {}
    for name, n in (("r", 512), ("k", 512), ("v", 512), ("xw", LORA_W), ("xa", LORA_A), ("xg", LORA_G),
                    ("q", 512), ("kf", 512), ("vf", 512), ("f", N_HEADS), ("gr", 1024), ("gf", 1024)):
        seg[name] = wt[o:o + n]
        o += n
    z = lambda n: jnp.zeros((n, D_MODEL), F32)
    lora = jnp.concatenate([seg["xw"], seg["xa"], seg["xg"], z(F_OFF - 288), seg["f"],
                            z(LORA_TILE - F_OFF - N_HEADS)], axis=0)
    fpad = jnp.concatenate([seg["f"], z(LANES - N_HEADS)], axis=0)
    wtok = jnp.concatenate([seg["gr"], seg["gf"], seg["r"], seg["k"], seg["v"], lora, seg["q"]], axis=0)
    wt_prompt = jnp.concatenate([seg["kf"], seg["vf"], fpad], axis=0)
    wt_sample = jnp.concatenate([seg["q"], seg["kf"], seg["vf"], fpad], axis=0)
    row = lambda x: x.reshape(1, -1).astype(F32)
    hid = jnp.arange(D_HEADS) // HEAD_DIM
    i128 = jnp.arange(LANES)
    upper_incl = (i128[:, None] <= i128[None, :])
    lower_incl = (i128[:, None] >= i128[None, :])
    ones = jnp.ones((LANES, LANES), bool)
    return dict(
        wtok=wtok.astype(BF16), wt_prompt=wt_prompt.astype(BF16), wt_sample=wt_sample.astype(BF16),
        mu=jnp.pad(row(mu), ((0, 0), (0, RW - RWKV_COLS))),
        w0=row(w0), a0=row(a0), k_k=row(k_k), k_a=row(k_a), r_k=row(r_k), lnx_g=row(lnx_g), lnx_b=row(lnx_b),
        w2p=jnp.pad(w2, ((0, LANES - LORA_W), (0, 0))).astype(BF16),
        a2p=jnp.pad(a2, ((LORA_W, 0), (0, 0))).astype(BF16),
        g2p=jnp.pad(g2, ((0, 2 * LANES - LORA_G), (0, 0))).astype(BF16),
        bones=(hid[:, None] == hid[None, :]).astype(BF16),
        tri=lower_incl.astype(BF16),
        bf_col=jnp.broadcast_to(b_f.reshape(N_HEADS, 1), (N_HEADS, LANES)).astype(F32),
        bf_row=jnp.pad(row(b_f), ((0, 0), (0, LANES - N_HEADS))),
        u_lane=jnp.concatenate([upper_incl, ones], axis=1).astype(BF16),
        l_sub=jnp.concatenate([lower_incl, ones], axis=0).astype(BF16),
        u_suffix=jnp.concatenate([i128[:, None] > i128[None, :], ones], axis=1).astype(BF16),
        w_br=w_br.astype(BF16), w_bf=w_bf.astype(BF16), w_o=w_o.astype(BF16),
        ln1_g=row(ln1_g), ln1_b=row(ln1_b), w_up=w_up.astype(BF16), w_down=w_down.astype(BF16),
        ln2_g=row(ln2_g), ln2_b=row(ln2_b))


def _unpack_shift(rows):
    return rows[:, :RWKV_COLS]


def _prompt_group(x_prompt, par):
    nb, t, _ = x_prompt.shape
    x = x_prompt.reshape(nb * t, D_MODEL)
    tok, tt = _project(x, par["wtok"], par["wt_prompt"], nb, 256)
    out_r, wkv, shift = _rwkv_prompt(tok, nb, par)
    lft, ct, ccol = _fox_gate(tok, tt, nb, par)
    o_f = _fox_attn(tok, tt, ct, ccol, nb, 256)
    h = _merge(x, tok, out_r, o_f, par, 256)
    y = _ffn(h, par, 512)
    kt = tt[:, 0:D_HEADS].reshape(nb, N_HEADS, HEAD_DIM, t)
    vt = tt[:, D_HEADS:2 * D_HEADS].reshape(nb, N_HEADS, HEAD_DIM, t)
    k_out = jnp.transpose(kt, (0, 3, 1, 2))[None]
    v_out = jnp.transpose(vt, (0, 3, 1, 2))[None]
    lf_out = jnp.transpose(lft, (0, 2, 1))[None]
    wkv_out = jnp.transpose(wkv.reshape(nb, N_PAIRS, HEAD_DIM, 2, HEAD_DIM), (0, 1, 3, 2, 4))
    wkv_out = wkv_out.reshape(1, nb, N_HEADS, HEAD_DIM, HEAD_DIM)
    return (y.reshape(nb, t, D_MODEL), k_out, v_out, lf_out, wkv_out,
            _unpack_shift(shift.reshape(nb, RW))[None])


def _sample_group(x_sample, cache_k, cache_v, cache_logf, page_table, state_wkv, state_shift, par):
    nb = x_sample.shape[0]
    x = x_sample.reshape(nb, D_MODEL)
    tok, tt = _project(x, par["wtok"], par["wt_sample"], 1, nb)
    prev = jnp.pad(state_shift, ((0, 0), (0, RW - RWKV_COLS)))
    state_t = jnp.transpose(state_wkv, (1, 2, 3, 0))[None]
    out_r, state_o = _rwkv_sample(tok, prev, state_t, par)
    ck = jnp.transpose(cache_k, (0, 2, 3, 1))[None]
    cv = jnp.transpose(cache_v, (0, 2, 3, 1))[None]
    clf = jnp.transpose(cache_logf, (0, 2, 1))[None]
    o_f, lfn = _fox_decode(tt, ck, cv, clf, page_table, par)
    h = _merge(x, tok, out_r, o_f, par, nb)
    y = _ffn(h, par, nb)
    kt = tt[0, D_HEADS:2 * D_HEADS].reshape(N_HEADS, HEAD_DIM, nb)
    vt = tt[0, 2 * D_HEADS:3 * D_HEADS].reshape(N_HEADS, HEAD_DIM, nb)
    k_out = jnp.transpose(kt, (2, 0, 1)).reshape(1, nb, 1, N_HEADS, HEAD_DIM)
    v_out = jnp.transpose(vt, (2, 0, 1)).reshape(1, nb, 1, N_HEADS, HEAD_DIM)
    lf_out = jnp.transpose(lfn, (1, 0)).reshape(1, nb, 1, N_HEADS)
    wkv_out = jnp.transpose(state_o[0], (3, 0, 1, 2))[None]
    shift_out = tok[:, COL_R:COL_R + RWKV_COLS][None]
    return y.reshape(nb, 1, D_MODEL), k_out, v_out, lf_out, wkv_out, shift_out


def kernel(x_prompt, x_sample, cache_k, cache_v, cache_logf, page_table, state_wkv, state_shift, w_in, mu, w0, w2, a0, a2, g2, k_k, k_a, r_k, lnx_g, lnx_b, b_f, w_br, w_bf, w_o, ln1_g, ln1_b, w_up, w_down, ln2_g, ln2_b):
    assert w_in.shape[0] == 1, "single-layer trunk"
    par = _prepare_params(w_in[0], mu[0], w0[0], w2[0], a0[0], a2[0], g2[0], k_k[0], k_a[0],
                          r_k[0].reshape(-1), lnx_g[0], lnx_b[0], b_f[0], w_br[0], w_bf[0], w_o[0],
                          ln1_g[0], ln1_b[0], w_up[0], w_down[0], ln2_g[0], ln2_b[0])
    yp, kp, vp, lfp, wp, sp = _prompt_group(x_prompt, par)
    ys, kd, vd, lfd, wd, sd = _sample_group(x_sample, cache_k[0], cache_v[0], cache_logf[0], page_table,
                                            state_wkv[0], state_shift[0], par)
    return (yp, ys, kp, vp, lfp, wp, sp, kd, vd, lfd, wd, sd)
```

```python
import functools

import jax
import jax.numpy as jnp
from jax import lax
from jax.experimental import pallas as pl
from jax.experimental.pallas import tpu as pltpu

F32 = jnp.float32
BF16 = jnp.bfloat16

D_MODEL = 1024
HEAD_DIM = 64
N_HEADS = 8
D_HEADS = N_HEADS * HEAD_DIM
N_PAIRS = N_HEADS // 2
LANES = 128
SUBLANES = 8
LORA_W, LORA_A, LORA_G = 64, 64, 160
RWKV_COLS = 3 * D_HEADS + LORA_W + LORA_A + LORA_G
D_FF = 2816
LN_EPS = 1e-5
GN_EPS = 64e-5
NEG = -1e30
SCALE = HEAD_DIM ** -0.5
ALPHA = 2.0 ** 0.25
CHUNK = 128
VMEM_LIMIT = 56 * 1024 * 1024

LORA_TILE = 512
F_OFF = 384
COL_GR, COL_GF, COL_R, COL_Q = 0, 1024, 2048, 4096
N_TOK = 4608
RW = 2048


def _dot(a, b):
    return jnp.dot(a, b, preferred_element_type=F32)


def _dot_nt(a, b):
    return lax.dot_general(a, b, (((1,), (1,)), ((), ())), preferred_element_type=F32)


def _dot_tn(a, b):
    return lax.dot_general(a, b, (((0,), (0,)), ((), ())), preferred_element_type=F32)


def _split3(x):
    hi = x.astype(BF16)
    r1 = x - hi.astype(F32)
    mid = r1.astype(BF16)
    lo = (r1 - mid.astype(F32)).astype(BF16)
    return hi, mid, lo


def _dot_x3(x, m):
    hi, mid, lo = _split3(x)
    return _dot(hi, m) + _dot(mid, m) + _dot(lo, m)


def _dot_3x(m, x):
    hi, mid, lo = _split3(x)
    return _dot(m, hi) + _dot(m, mid) + _dot(m, lo)


def _sigmoid(z):
    return 1.0 / (1.0 + jnp.exp(-z))


def _softplus(z):
    return jnp.maximum(z, 0.0) + jnp.log1p(jnp.exp(-jnp.abs(z)))


def _log_sigmoid(z):
    return -_softplus(-z)


def _cparams(sem):
    return pltpu.CompilerParams(dimension_semantics=sem, vmem_limit_bytes=VMEM_LIMIT)


def _const_spec(shape):
    nd = len(shape)
    return pl.BlockSpec(shape, lambda *_: (0,) * nd, pipeline_mode=pl.Buffered(1))


def _proj_kernel(x_ref, wtok_ref, wt_ref, tok_ref, t_ref, *, tn):
    xb = x_ref[...].astype(BF16)
    for j in range(wtok_ref.shape[0] // tn):
        tok_ref[:, j * tn:(j + 1) * tn] = _dot_nt(xb, wtok_ref[j * tn:(j + 1) * tn, :])
    t_ref[0] = _dot_nt(wt_ref[...], xb)


def _project(x, wtok, wt, nb, tm):
    m = x.shape[0]
    t = m // nb
    nt = t // tm
    return pl.pallas_call(
        functools.partial(_proj_kernel, tn=512),
        grid=(m // tm,),
        in_specs=[pl.BlockSpec((tm, D_MODEL), lambda i: (i, 0)),
                  _const_spec(wtok.shape), _const_spec(wt.shape)],
        out_specs=[pl.BlockSpec((tm, wtok.shape[0]), lambda i: (i, 0)),
                   pl.BlockSpec((1, wt.shape[0], tm), lambda i: (i // nt, 0, i % nt))],
        out_shape=[jax.ShapeDtypeStruct((m, wtok.shape[0]), F32),
                   jax.ShapeDtypeStruct((nb, wt.shape[0], t), F32)],
        compiler_params=_cparams(("arbitrary",)),
        name="proj",
    )(x, wtok, wt)


def _rwkv_prep(p, prev, mu, w0, a0, k_k, k_a, w2p, a2p, g2p, bones):
    xx = p + (prev - p) * mu
    r = xx[:, 0:512]
    k = xx[:, 512:1024]
    v = xx[:, 1024:1536]
    t0 = xx[:, 1536:1664]
    t12 = xx[:, 1664:1920]
    w = -_softplus(-(w0 + _dot(jnp.tanh(t0).astype(BF16), w2p))) - 0.5
    log_w = -jnp.exp(w)
    a = _sigmoid(a0 + _dot(t0.astype(BF16), a2p))
    g = _dot(_sigmoid(t12).astype(BF16), g2p)
    kk = k * k_k
    ss = _dot_x3(kk * kk, bones)
    kk = kk * lax.rsqrt(jnp.maximum(ss, 1e-24))
    k2 = k * (1.0 + (a - 1.0) * k_a)
    return r, k2, v, kk, a, log_w, g


def _rwkv_post(y, r, k2, v, g, r_k, lnx_g, lnx_b, bones):
    inv = 1.0 / HEAD_DIM
    mean = _dot_x3(y, bones) * inv
    d = y - mean
    var = _dot_x3(d * d, bones) * inv
    yn = d * lax.rsqrt(var + GN_EPS) * lnx_g + lnx_b
    bonus = _dot_x3(r * k2 * r_k, bones) * v
    return (yn + bonus) * g


def _rwkv_chunk_kernel(p_ref, mu_ref, w0_ref, a0_ref, kk_ref, ka_ref, rk_ref, lng_ref, lnb_ref,
                       w2_ref, a2_ref, g2_ref, bones_ref, tri_ref,
                       out_ref, wkv_ref, shift_ref,
                       prev_s, wl_s, st_s, at_s, rt_s, ar_s, rr_s, br_s, kr_s, v_s, be_s, ke_s, y_s):
    c = pl.program_id(1)
    L = CHUNK

    @pl.when(c == 0)
    def _():
        prev_s[...] = jnp.zeros_like(prev_s)
        st_s[...] = jnp.zeros_like(st_s)

    p = p_ref[...]
    row = lax.broadcasted_iota(jnp.int32, (L, RW), 0)
    prev = jnp.where(row == 0, prev_s[...], pltpu.roll(p, 1, axis=0))
    last = p_ref[L - 1:L, :]
    prev_s[...] = last
    bones = bones_ref[...]
    r, k2, v, kk, a, lw, g = _rwkv_prep(p, prev, mu_ref[...], w0_ref[...], a0_ref[...], kk_ref[...],
                                        ka_ref[...], w2_ref[...], a2_ref[...], g2_ref[...], bones)
    b = kk * a
    gam = _dot_3x(tri_ref[...], lw)
    y_s[...] = gam
    gmid = y_s[L // 2 - 1:L // 2, :]
    glast = y_s[L - 1:L, :]
    e_mid_inv = jnp.exp(-gmid)
    a_true = -kk * jnp.exp(gam - lw)
    r_true = r * jnp.exp(gam)
    e_mg = jnp.exp(gmid - gam)
    e_end = jnp.exp(glast - gam)
    wl_s[...] = jnp.exp(glast)
    at_s[...] = a_true
    rt_s[...] = r_true
    ar_s[...] = a_true * e_mid_inv
    rr_s[...] = r_true * e_mid_inv
    br_s[...] = b * e_mg
    kr_s[...] = k2 * e_mg
    v_s[...] = v
    be_s[...] = b * e_end
    ke_s[...] = k2 * e_end

    lane = lax.broadcasted_iota(jnp.int32, (L, LANES), 1)
    rowl = lax.broadcasted_iota(jnp.int32, (L, LANES), 0)
    lo = lane < HEAD_DIM
    strict = rowl > lane
    incl = rowl >= lane
    lo64 = lax.broadcasted_iota(jnp.int32, (HEAD_DIM, LANES), 1) < HEAD_DIM
    n_dbl = 7

    pairs = []
    for pp in range(N_PAIRS):
        sl = slice(pp * LANES, (pp + 1) * LANES)
        a_t = at_s[:, sl]
        r_t = rt_s[:, sl]
        pairs.append(dict(
            sl=sl, a_r=ar_s[:, sl], r_r=rr_s[:, sl], vb=v_s[:, sl].astype(BF16),
            rhs=jnp.concatenate([br_s[:, sl], kr_s[:, sl]], axis=0).astype(BF16),
            be=be_s[:, sl].astype(BF16), ke=ke_s[:, sl].astype(BF16),
            a_sw=pltpu.roll(a_t, HEAD_DIM, axis=1), r_sw=pltpu.roll(r_t, HEAD_DIM, axis=1)))
    heads = [(pp, hh) for pp in range(N_PAIRS) for hh in range(2)]
    gms = []
    for pp, hh in heads:
        d = pairs[pp]
        msk = lo if hh == 0 else jnp.logical_not(lo)
        lhs = jnp.concatenate([jnp.where(msk, d["a_r"], 0.0), jnp.where(msk, d["r_r"], 0.0)],
                              axis=0).astype(BF16)
        gms.append(_dot_nt(lhs, d["rhs"]))
    pms = [jnp.where(strict, gm[:L, :L], 0.0).astype(BF16) for gm in gms]
    aaks = [jnp.where(strict, gm[:L, L:], 0.0).astype(BF16) for gm in gms]
    arbs = [jnp.where(incl, gm[L:, :L], 0.0).astype(BF16) for gm in gms]
    arks = [jnp.where(incl, gm[L:, L:], 0.0).astype(BF16) for gm in gms]
    zs = []
    for i, (pp, hh) in enumerate(heads):
        d = pairs[pp]
        av = _dot(aaks[i], d["vb"])
        zs.append(jnp.where(lo, av, d["a_sw"]) if hh == 0 else jnp.where(lo, d["a_sw"], av))
    for it in range(n_dbl):
        zs = [z + _dot(pm, z.astype(BF16)) for z, pm in zip(zs, pms)]
        if it < n_dbl - 1:
            pms = [_dot(pm, pm).astype(BF16) for pm in pms]
    zbs = [z.astype(BF16) for z in zs]
    rys = []
    qqs = []
    for i, (pp, hh) in enumerate(heads):
        d = pairs[pp]
        avk = _dot(arks[i], d["vb"])
        tail = jnp.where(lo, avk, d["r_sw"]) if hh == 0 else jnp.where(lo, d["r_sw"], avk)
        rys.append(_dot(arbs[i], zbs[i]) + tail)
        qqs.append(_dot_tn(zbs[i], d["be"]))
    for pp in range(N_PAIRS):
        d = pairs[pp]
        sl = d["sl"]
        ry = rys[2 * pp:2 * pp + 2]
        qq = qqs[2 * pp:2 * pp + 2]
        vk = _dot_tn(d["vb"], d["ke"])
        rcomb = jnp.where(lo, ry[1], ry[0])
        y0 = jnp.where(lo, ry[0], ry[1])
        wl_p = wl_s[:, sl]
        mpair = jnp.concatenate([jnp.where(lo64, qq[0][HEAD_DIM:], 0.0),
                                 jnp.where(lo64, 0.0, qq[1][:HEAD_DIM])], axis=0)
        mpair = mpair + jnp.where(rowl == lane, wl_p, 0.0)
        npair = jnp.where(lo64, qq[0][:HEAD_DIM] + vk[:HEAD_DIM], qq[1][HEAD_DIM:] + vk[HEAD_DIM:])
        sp = st_s[pp]
        ssw = pltpu.roll(sp, HEAD_DIM, axis=1)
        santi = jnp.concatenate([jnp.where(lo64, 0.0, ssw), jnp.where(lo64, ssw, 0.0)], axis=0)
        y_s[:, sl] = _dot_nt(rcomb.astype(BF16), santi.astype(BF16)) + y0
        st_s[pp] = _dot(sp.astype(BF16), mpair.astype(BF16)) + npair

    out_ref[...] = _rwkv_post(y_s[...], r, k2, v, g, rk_ref[...], lng_ref[...], lnb_ref[...], bones)

    @pl.when(c == pl.num_programs(1) - 1)
    def _():
        wkv_ref[0] = st_s[...]
        shift_ref[0] = last


def _rwkv_prompt(tok, nb, par):
    m = tok.shape[0]
    nc = m // nb // CHUNK
    vec = pl.BlockSpec((1, D_HEADS), lambda b, c: (0, 0))
    return pl.pallas_call(
        _rwkv_chunk_kernel,
        grid=(nb, nc),
        in_specs=[pl.BlockSpec((CHUNK, RW), lambda b, c: (b * nc + c, COL_R // RW)),
                  pl.BlockSpec((1, RW), lambda b, c: (0, 0)),
                  vec, vec, vec, vec, vec, vec, vec,
                  pl.BlockSpec((LANES, D_HEADS), lambda b, c: (0, 0)),
                  pl.BlockSpec((LANES, D_HEADS), lambda b, c: (0, 0)),
                  pl.BlockSpec((2 * LANES, D_HEADS), lambda b, c: (0, 0)),
                  pl.BlockSpec((D_HEADS, D_HEADS), lambda b, c: (0, 0)),
                  pl.BlockSpec((CHUNK, CHUNK), lambda b, c: (0, 0))],
        out_specs=[pl.BlockSpec((CHUNK, D_HEADS), lambda b, c: (b * nc + c, 0)),
                   pl.BlockSpec((1, N_PAIRS, HEAD_DIM, LANES), lambda b, c: (b, 0, 0, 0)),
                   pl.BlockSpec((1, 1, RW), lambda b, c: (b, 0, 0))],
        out_shape=[jax.ShapeDtypeStruct((m, D_HEADS), F32),
                   jax.ShapeDtypeStruct((nb, N_PAIRS, HEAD_DIM, LANES), F32),
                   jax.ShapeDtypeStruct((nb, 1, RW), F32)],
        scratch_shapes=[pltpu.VMEM((1, RW), F32), pltpu.VMEM((1, D_HEADS), F32),
                        pltpu.VMEM((N_PAIRS, HEAD_DIM, LANES), F32)]
                       + [pltpu.VMEM((CHUNK, D_HEADS), F32)] * 10,
        compiler_params=_cparams(("arbitrary", "arbitrary")),
        name="rwkv_chunk",
    )(tok, par["mu"], par["w0"], par["a0"], par["k_k"], par["k_a"], par["r_k"], par["lnx_g"],
      par["lnx_b"], par["w2p"], par["a2p"], par["g2p"], par["bones"], par["tri"])


def _fox_gate_kernel(ft_ref, ftok_ref, bfc_ref, bfr_ref, ul_ref, ll_ref, lft_ref, ct_ref, ccol_ref):
    t = ft_ref.shape[2]
    lft = _log_sigmoid(ft_ref[0][:N_HEADS, :] + bfc_ref[:, 0:1])
    lft_ref[0] = lft
    carry = jnp.zeros((N_HEADS, LANES), F32)
    for blk in range(t // LANES):
        cs = _dot_x3(lft[:, blk * LANES:(blk + 1) * LANES], ul_ref[...])
        ct_ref[0, :, blk * LANES:(blk + 1) * LANES] = cs[:, :LANES] + carry
        carry = carry + cs[:, LANES:]
    carry_r = jnp.zeros((LANES, LANES), F32)
    for blk in range(t // LANES):
        lf = _log_sigmoid(ftok_ref[blk * LANES:(blk + 1) * LANES, :] + bfr_ref[...])
        cs = _dot_3x(ll_ref[...], lf)
        ccol_ref[blk * LANES:(blk + 1) * LANES, :] = cs[:LANES] + carry_r
        carry_r = carry_r + cs[LANES:]


def _fox_gate(tok, tt, nb, par):
    m = tok.shape[0]
    t = m // nb
    fcol = (COL_R + 3 * D_HEADS + F_OFF) // LANES
    frow = (2 * D_HEADS) // LANES
    return pl.pallas_call(
        _fox_gate_kernel,
        grid=(nb,),
        in_specs=[pl.BlockSpec((1, LANES, t), lambda b: (b, frow, 0)),
                  pl.BlockSpec((t, LANES), lambda b: (b, fcol)),
                  pl.BlockSpec((N_HEADS, LANES), lambda b: (0, 0)),
                  pl.BlockSpec((1, LANES), lambda b: (0, 0)),
                  pl.BlockSpec((LANES, 2 * LANES), lambda b: (0, 0)),
                  pl.BlockSpec((2 * LANES, LANES), lambda b: (0, 0))],
        out_specs=[pl.BlockSpec((1, N_HEADS, t), lambda b: (b, 0, 0)),
                   pl.BlockSpec((1, N_HEADS, t), lambda b: (b, 0, 0)),
                   pl.BlockSpec((t, LANES), lambda b: (b, 0))],
        out_shape=[jax.ShapeDtypeStruct((nb, N_HEADS, t), F32),
                   jax.ShapeDtypeStruct((nb, N_HEADS, t), F32),
                   jax.ShapeDtypeStruct((m, LANES), F32)],
        compiler_params=_cparams(("arbitrary",)),
        name="fox_gate",
    )(tt, tok, par["bf_col"], par["bf_row"], par["u_lane"], par["l_sub"])


ATTN_GROUP = 8


def _fox_attn_kernel(q_ref, kt_ref, vt_ref, ct_ref, ccol_ref, o_ref,
                     kb_s, vb_s, qh_s, cq_s, m_s, l_s, acc_s, *, tq):
    qi = pl.program_id(1)

    @pl.when(qi == 0)
    def _():
        kb_s[...] = kt_ref[0].astype(BF16)
        vb_s[...] = vt_ref[0].astype(BF16)

    lane = lax.broadcasted_iota(jnp.int32, (tq, LANES), 1)
    lo = lane < HEAD_DIM
    rowq = lax.broadcasted_iota(jnp.int32, (tq, tq), 0)
    colq = lax.broadcasted_iota(jnp.int32, (tq, tq), 1)
    causal = colq <= rowq
    ccol = ccol_ref[...]

    for pp in range(N_PAIRS):
        q = q_ref[:, pp * LANES:(pp + 1) * LANES] * SCALE
        qh_s[2 * pp] = jnp.where(lo, q, 0.0).astype(BF16)
        qh_s[2 * pp + 1] = jnp.where(lo, 0.0, q).astype(BF16)
    for h in range(N_HEADS):
        cq_s[h] = ccol[:, h:h + 1]
    m_s[...] = jnp.full_like(m_s, NEG)
    l_s[...] = jnp.zeros_like(l_s)
    acc_s[...] = jnp.zeros_like(acc_s)

    def step_group(off, heads, masked):
        ss = {}
        for h in heads:
            rs = slice((h // 2) * LANES, (h // 2 + 1) * LANES)
            s = _dot(qh_s[h], kb_s[rs, pl.ds(off, tq)]) + cq_s[h] - ct_ref[0, h:h + 1, pl.ds(off, tq)]
            ss[h] = jnp.where(causal, s, NEG) if masked else s
        m_old = {h: m_s[h] for h in heads}
        m_new = {h: jnp.maximum(m_old[h], jnp.max(ss[h], axis=1, keepdims=True)) for h in heads}
        alpha = {h: jnp.exp(m_old[h] - m_new[h]) for h in heads}
        pes = {h: jnp.exp(ss[h] - m_new[h]) for h in heads}
        for h in heads:
            m_s[h] = m_new[h]
            l_s[h] = l_s[h] * alpha[h] + jnp.sum(pes[h], axis=1, keepdims=True)
        for h in heads:
            rs = slice((h // 2) * LANES, (h // 2 + 1) * LANES)
            acc_s[h] = acc_s[h] * alpha[h] + _dot_nt(pes[h].astype(BF16), vb_s[rs, pl.ds(off, tq)])

    def step(j, masked):
        off = pl.multiple_of(j * tq, tq)
        for g0 in range(0, N_HEADS, ATTN_GROUP):
            step_group(off, range(g0, g0 + ATTN_GROUP), masked)

    def body(j, carry):
        step(j, False)
        return carry

    lax.fori_loop(0, qi, body, 0)
    step(qi, True)
    for pp in range(N_PAIRS):
        oa = acc_s[2 * pp] / l_s[2 * pp]
        ob = acc_s[2 * pp + 1] / l_s[2 * pp + 1]
        o_ref[:, pp * LANES:(pp + 1) * LANES] = jnp.where(lo, oa, ob)


def _fox_attn(tok, tt, ct, ccol, nb, tq):
    m = tok.shape[0]
    t = m // nb
    nq = t // tq
    return pl.pallas_call(
        functools.partial(_fox_attn_kernel, tq=tq),
        grid=(nb, nq),
        in_specs=[pl.BlockSpec((tq, D_HEADS), lambda b, i: (b * nq + i, COL_Q // D_HEADS)),
                  pl.BlockSpec((1, D_HEADS, t), lambda b, i: (b, 0, 0)),
                  pl.BlockSpec((1, D_HEADS, t), lambda b, i: (b, 1, 0)),
                  pl.BlockSpec((1, N_HEADS, t), lambda b, i: (b, 0, 0)),
                  pl.BlockSpec((tq, LANES), lambda b, i: (b * nq + i, 0))],
        out_specs=pl.BlockSpec((tq, D_HEADS), lambda b, i: (b * nq + i, 0)),
        out_shape=jax.ShapeDtypeStruct((m, D_HEADS), F32),
        scratch_shapes=[pltpu.VMEM((D_HEADS, t), BF16), pltpu.VMEM((D_HEADS, t), BF16),
                        pltpu.VMEM((N_HEADS, tq, LANES), BF16), pltpu.VMEM((N_HEADS, tq, 1), F32),
                        pltpu.VMEM((N_HEADS, tq, 1), F32), pltpu.VMEM((N_HEADS, tq, 1), F32),
                        pltpu.VMEM((N_HEADS, tq, LANES), F32)],
        compiler_params=_cparams(("arbitrary", "arbitrary")),
        name="fox_attn",
    )(tok, tt, tt, ct, ccol)


def _layer_norm(z, g, b):
    mu = jnp.mean(z, axis=-1, keepdims=True)
    d = z - mu
    var = jnp.mean(d * d, axis=-1, keepdims=True)
    return d * lax.rsqrt(var + LN_EPS) * g + b


def _merge_kernel(x_ref, gr_ref, gf_ref, r_ref, f_ref, wbr_ref, wbf_ref, wo_ref, g_ref, b_ref, h_ref):
    a = _dot(r_ref[...].astype(BF16), wbr_ref[...])
    b = _dot(f_ref[...].astype(BF16), wbf_ref[...])
    mixed = _sigmoid(gr_ref[...]) * a + _sigmoid(gf_ref[...]) * b
    z = ALPHA * x_ref[...] + _dot(mixed.astype(BF16), wo_ref[...])
    h_ref[...] = _layer_norm(z, g_ref[...], b_ref[...])


def _merge(x, tok, out_r, o_f, par, tm):
    m = x.shape[0]
    row = lambda c: pl.BlockSpec((tm, c), lambda i: (i, 0))
    return pl.pallas_call(
        _merge_kernel,
        grid=(m // tm,),
        in_specs=[row(D_MODEL),
                  pl.BlockSpec((tm, D_MODEL), lambda i: (i, COL_GR // D_MODEL)),
                  pl.BlockSpec((tm, D_MODEL), lambda i: (i, COL_GF // D_MODEL)),
                  row(D_HEADS), row(D_HEADS),
                  _const_spec((D_HEADS, D_MODEL)), _const_spec((D_HEADS, D_MODEL)),
                  _const_spec((D_MODEL, D_MODEL)), _const_spec((1, D_MODEL)), _const_spec((1, D_MODEL))],
        out_specs=row(D_MODEL),
        out_shape=jax.ShapeDtypeStruct((m, D_MODEL), F32),
        compiler_params=_cparams(("arbitrary",)),
        name="merge",
    )(x, tok, tok, out_r, o_f, par["w_br"], par["w_bf"], par["w_o"], par["ln1_g"], par["ln1_b"])


FF_CHUNK = 256


def _ffn_kernel(h_ref, wup_ref, wdn_ref, g_ref, b_ref, y_ref):
    h = h_ref[...]
    hb = h.astype(BF16)
    acc = ALPHA * h
    for ci in range(D_FF // FF_CHUNK):
        cs = slice(ci * FF_CHUNK, (ci + 1) * FF_CHUNK)
        ug = _dot(hb, wup_ref[:, cs])
        uv = _dot(hb, wup_ref[:, D_FF + ci * FF_CHUNK:D_FF + (ci + 1) * FF_CHUNK])
        act = (ug * _sigmoid(ug) * uv).astype(BF16)
        acc = acc + _dot(act, wdn_ref[cs, :])
    y_ref[...] = _layer_norm(acc, g_ref[...], b_ref[...])


def _ffn(h, par, tm):
    m = h.shape[0]
    return pl.pallas_call(
        _ffn_kernel,
        grid=(m // tm,),
        in_specs=[pl.BlockSpec((tm, D_MODEL), lambda i: (i, 0)),
                  _const_spec((D_MODEL, 2 * D_FF)), _const_spec((D_FF, D_MODEL)),
                  _const_spec((1, D_MODEL)), _const_spec((1, D_MODEL))],
        out_specs=pl.BlockSpec((tm, D_MODEL), lambda i: (i, 0)),
        out_shape=jax.ShapeDtypeStruct((m, D_MODEL), F32),
        compiler_params=_cparams(("arbitrary",)),
        name="ffn",
    )(h, par["w_up"], par["w_down"], par["ln2_g"], par["ln2_b"])


def _rwkv_step_kernel(p_ref, prev_ref, mu_ref, w0_ref, a0_ref, kk_ref, ka_ref, rk_ref, lng_ref, lnb_ref,
                      w2_ref, a2_ref, g2_ref, bones_ref, s_ref,
                      out_ref, so_ref,
                      r_s, k_s, v_s, g_s, rt_s, kt_s, vt_s, nk_s, bt_s, dt_s, yt_s):
    h = pl.program_id(0)

    @pl.when(h == 0)
    def _():
        r, k2, v, kk, a, lw, g = _rwkv_prep(p_ref[...], prev_ref[...], mu_ref[...], w0_ref[...],
                                            a0_ref[...], kk_ref[...], ka_ref[...], w2_ref[...],
                                            a2_ref[...], g2_ref[...], bones_ref[...])
        r_s[...] = r
        k_s[...] = k2
        v_s[...] = v
        g_s[...] = g
        rt_s[...] = r.T
        kt_s[...] = k2.T
        vt_s[...] = v.T
        nk_s[...] = (-kk).T
        bt_s[...] = (kk * a).T
        dt_s[...] = jnp.exp(lw).T

    hs = pl.ds(pl.multiple_of(h * HEAD_DIM, HEAD_DIM), HEAD_DIM)
    nkk = nk_s[hs, :]
    dec = dt_s[hs, :]
    bb = bt_s[hs, :]
    kk2 = kt_s[hs, :]
    rr = rt_s[hs, :]

    def body(vi, _):
        s = s_ref[0, 0, vi]
        sa = jnp.sum(s * nkk, axis=0, keepdims=True)
        vrow = vt_s[pl.ds(h * HEAD_DIM + vi, 1), :]
        s2 = s * dec + sa * bb + vrow * kk2
        so_ref[0, 0, vi] = s2
        yt_s[pl.ds(h * HEAD_DIM + vi, 1), :] = jnp.sum(s2 * rr, axis=0, keepdims=True)
        return 0

    lax.fori_loop(0, HEAD_DIM, body, 0)

    @pl.when(h == N_HEADS - 1)
    def _():
        out_ref[...] = _rwkv_post(yt_s[...].T, r_s[...], k_s[...], v_s[...], g_s[...], rk_ref[...],
                                  lng_ref[...], lnb_ref[...], bones_ref[...])


def _rwkv_sample(tok, prev, state_t, par):
    nb = tok.shape[0]
    vec = pl.BlockSpec((1, D_HEADS), lambda h: (0, 0))
    sspec = pl.BlockSpec((1, 1, HEAD_DIM, HEAD_DIM, nb), lambda h: (0, h, 0, 0, 0))
    return pl.pallas_call(
        _rwkv_step_kernel,
        grid=(N_HEADS,),
        in_specs=[pl.BlockSpec((nb, RW), lambda h: (0, COL_R // RW)),
                  pl.BlockSpec((nb, RW), lambda h: (0, 0)),
                  pl.BlockSpec((1, RW), lambda h: (0, 0)),
                  vec, vec, vec, vec, vec, vec, vec,
                  pl.BlockSpec((LANES, D_HEADS), lambda h: (0, 0)),
                  pl.BlockSpec((LANES, D_HEADS), lambda h: (0, 0)),
                  pl.BlockSpec((2 * LANES, D_HEADS), lambda h: (0, 0)),
                  pl.BlockSpec((D_HEADS, D_HEADS), lambda h: (0, 0)),
                  sspec],
        out_specs=[pl.BlockSpec((nb, D_HEADS), lambda h: (0, 0)), sspec],
        out_shape=[jax.ShapeDtypeStruct((nb, D_HEADS), F32),
                   jax.ShapeDtypeStruct(state_t.shape, F32)],
        scratch_shapes=[pltpu.VMEM((nb, D_HEADS), F32)] * 4 + [pltpu.VMEM((D_HEADS, nb), F32)] * 7,
        compiler_params=_cparams(("arbitrary",)),
        name="rwkv_step",
    )(tok, prev, par["mu"], par["w0"], par["a0"], par["k_k"], par["k_a"], par["r_k"], par["lnx_g"],
      par["lnx_b"], par["w2p"], par["a2p"], par["g2p"], par["bones"], state_t)


PAGE_GROUP = 8
DECODE_SLOTS = 3


def _allreduce_sublanes(x):
    x = x + pltpu.roll(x, 4, axis=0)
    x = x + pltpu.roll(x, 2, axis=0)
    return x + pltpu.roll(x, 1, axis=0)


def _fold8(x):
    return jnp.sum(x.reshape(HEAD_DIM // SUBLANES, SUBLANES, LANES), axis=0)


def _tile8(x):
    return jnp.broadcast_to(x[None], (HEAD_DIM // SUBLANES, SUBLANES, LANES)).reshape(HEAD_DIM, LANES)


def _fox_decode_kernel(pt_ref, qt_ref, kn_ref, vn_ref, ft_ref, bfc_ref, us_ref, ck_ref, cv_ref, clf_ref,
                       o_ref, lfn_ref, kbuf, vbuf, lfbuf, sem_kv, sem_lf,
                       qb_s, m_s, l_s, c_s, acc_s, ot_s, *, n_groups, n_steps):
    pg = PAGE_GROUP
    b = pl.program_id(0)
    j = pl.program_id(1)
    nb = pl.num_programs(0)
    g = b * n_groups + j

    def group_copies(gi):
        if isinstance(gi, int):
            bi, ji, slot = gi // n_groups, gi % n_groups, gi % DECODE_SLOTS
        else:
            bi, ji, slot = lax.div(gi, n_groups), lax.rem(gi, n_groups), lax.rem(gi, DECODE_SLOTS)
        out = []
        for i in range(pg):
            page = pt_ref[bi, (n_groups - 1 - ji) * pg + i]
            out.append((pltpu.make_async_copy(ck_ref.at[0, page], kbuf.at[slot, i], sem_kv.at[slot]), i % 2))
            out.append((pltpu.make_async_copy(cv_ref.at[0, page], vbuf.at[slot, i], sem_kv.at[slot]),
                        (i + 1) % 2))
            out.append((pltpu.make_async_copy(clf_ref.at[0, page], lfbuf.at[slot, i], sem_lf.at[slot]), 0))
        return out

    @pl.when(g == 0)
    def _():
        for gi in range(DECODE_SLOTS - 1):
            for cp, prio in group_copies(gi):
                cp.start(priority=prio)

    @pl.when(g + (DECODE_SLOTS - 1) < n_steps)
    def _():
        for cp, prio in group_copies(g + (DECODE_SLOTS - 1)):
            cp.start(priority=prio)

    for cp, _ in group_copies(g):
        cp.wait()
    slot = lax.rem(g, DECODE_SLOTS)
    rowsel = lax.broadcasted_iota(jnp.int32, (LANES, LANES), 0)
    lanes8 = lax.broadcasted_iota(jnp.int32, (SUBLANES, LANES), 1)
    lfn_all = _log_sigmoid(ft_ref[0][:N_HEADS, :] + bfc_ref[:, 0:1])

    @pl.when(j == 0)
    def _():
        sel = (rowsel == b).astype(BF16)
        qb_s[...] = _dot_x3(qt_ref[0], sel) * SCALE
        cq = _dot_x3(lfn_all, sel)
        for hh in range(N_HEADS):
            c_s[hh] = jnp.broadcast_to(cq[hh:hh + 1, :], (SUBLANES, LANES))
        m_s[...] = jnp.full_like(m_s, NEG)
        l_s[...] = jnp.zeros_like(l_s)
        acc_s[...] = jnp.zeros_like(acc_s)

    @pl.when(jnp.logical_and(b == 0, j == 0))
    def _():
        lfn_ref[...] = lfn_all
        ot_s[...] = jnp.zeros_like(ot_s)

    su_all = _dot_x3(lfbuf[slot].reshape(pg * N_HEADS, LANES), us_ref[...])
    sus = [su_all[i * N_HEADS:(i + 1) * N_HEADS] for i in range(pg)]
    for hh in range(N_HEADS):
        hs = slice(hh * HEAD_DIM, (hh + 1) * HEAD_DIM)
        qh = qb_s[hs, :]
        run = c_s[hh]
        scores = [None] * pg
        for i in range(pg - 1, -1, -1):
            bias = jnp.broadcast_to(sus[i][hh:hh + 1, :LANES], (SUBLANES, LANES)) + run
            scores[i] = _allreduce_sublanes(_fold8(kbuf[slot, i, hh] * qh)) + bias
            run = run + jnp.broadcast_to(sus[i][hh:hh + 1, LANES:], (SUBLANES, LANES))
        c_s[hh] = run
        m_old = m_s[hh]
        m_new = m_old
        for s in scores:
            m_new = jnp.maximum(m_new, s)
        alpha = jnp.exp(m_old - m_new)
        pes = [jnp.exp(s - m_new) for s in scores]
        m_s[hh] = m_new
        l_new = l_s[hh] * alpha
        acc = acc_s[hs, :] * _tile8(alpha)
        for i in range(pg):
            l_new = l_new + pes[i]
            acc = acc + vbuf[slot, i, hh] * _tile8(pes[i])
        l_s[hh] = l_new
        acc_s[hs, :] = acc

    @pl.when(j == n_groups - 1)
    def _():
        sel = (rowsel == b).astype(BF16)
        knb = _dot_x3(kn_ref[0], sel)
        vnb = _dot_x3(vn_ref[0], sel)
        for hh in range(N_HEADS):
            hs = slice(hh * HEAD_DIM, (hh + 1) * HEAD_DIM)
            m = m_s[hh]
            mx = jnp.max(m, axis=1, keepdims=True)
            wgt = jnp.exp(m - mx)
            l_tot = jnp.sum(l_s[hh] * wgt, axis=1, keepdims=True)
            o_col = jnp.sum(acc_s[hs, :] * _tile8(wgt), axis=1, keepdims=True)
            s_new = _allreduce_sublanes(_fold8(knb[hs, :] * qb_s[hs, :]))
            m2 = jnp.maximum(mx, s_new)
            e1 = jnp.exp(mx - m2)
            e2 = jnp.exp(s_new - m2)
            inv = 1.0 / (l_tot * e1 + e2)
            o_h = (o_col * _tile8(e1) + vnb[hs, :] * _tile8(e2)) * _tile8(inv)
            lane64 = lax.broadcasted_iota(jnp.int32, (HEAD_DIM, LANES), 1)
            ot_s[hs, :] = jnp.where(lane64 == b, o_h, ot_s[hs, :])

    @pl.when(jnp.logical_and(b == nb - 1, j == n_groups - 1))
    def _():
        o_ref[...] = ot_s[...].T


def _fox_decode(tt, ck, cv, clf, page_table, par):
    nb = tt.shape[2]
    n_pages = page_table.shape[1]
    pg = PAGE_GROUP
    ng = n_pages // pg

    assert nb * ng >= DECODE_SLOTS - 1
    trow = lambda r, n: pl.BlockSpec((1, n, nb), lambda b, j, pt: (0, r, 0))
    hbm = pl.BlockSpec(memory_space=pl.ANY)
    in_specs = [trow(0, D_HEADS), trow(1, D_HEADS), trow(2, D_HEADS),
                pl.BlockSpec((1, LANES, nb), lambda b, j, pt: (0, 3 * D_HEADS // LANES, 0)),
                pl.BlockSpec((N_HEADS, LANES), lambda b, j, pt: (0, 0)),
                pl.BlockSpec((LANES, 2 * LANES), lambda b, j, pt: (0, 0)),
                hbm, hbm, hbm]
    return pl.pallas_call(
        functools.partial(_fox_decode_kernel, n_groups=ng, n_steps=nb * ng),
        grid_spec=pltpu.PrefetchScalarGridSpec(
            num_scalar_prefetch=1,
            grid=(nb, ng),
            in_specs=in_specs,
            out_specs=[pl.BlockSpec((nb, D_HEADS), lambda b, j, pt: (0, 0)),
                       pl.BlockSpec((N_HEADS, nb), lambda b, j, pt: (0, 0))],
            scratch_shapes=[pltpu.VMEM((DECODE_SLOTS, pg, N_HEADS, HEAD_DIM, LANES), F32),
                            pltpu.VMEM((DECODE_SLOTS, pg, N_HEADS, HEAD_DIM, LANES), F32),
                            pltpu.VMEM((DECODE_SLOTS, pg, N_HEADS, LANES), F32),
                            pltpu.SemaphoreType.DMA((DECODE_SLOTS,)),
                            pltpu.SemaphoreType.DMA((DECODE_SLOTS,)),
                            pltpu.VMEM((D_HEADS, LANES), F32),
                            pltpu.VMEM((N_HEADS, SUBLANES, LANES), F32),
                            pltpu.VMEM((N_HEADS, SUBLANES, LANES), F32),
                            pltpu.VMEM((N_HEADS, SUBLANES, LANES), F32),
                            pltpu.VMEM((D_HEADS, LANES), F32),
                            pltpu.VMEM((D_HEADS, nb), F32)]),
        out_shape=[jax.ShapeDtypeStruct((nb, D_HEADS), F32),
                   jax.ShapeDtypeStruct((N_HEADS, nb), F32)],
        compiler_params=_cparams(("arbitrary", "arbitrary")),
        name="fox_decode",
    )(page_table, tt, tt, tt, tt, par["bf_col"], par["u_suffix"], ck, cv, clf)


def _prepare_params(w_in, mu, w0, w2, a0, a2, g2, k_k, k_a, r_k, lnx_g, lnx_b, b_f,
                    w_br, w_bf, w_o, ln1_g, ln1_b, w_up, w_down, ln2_g, ln2_b):
    wt = w_in.T
    o = 0
    seg = {}
    for name, n in (("r", 512), ("k", 512), ("v", 512), ("xw", LORA_W), ("xa", LORA_A), ("xg", LORA_G),
                    ("q", 512), ("kf", 512), ("vf", 512), ("f", N_HEADS), ("gr", 1024), ("gf", 1024)):
        seg[name] = wt[o:o + n]
        o += n
    z = lambda n: jnp.zeros((n, D_MODEL), F32)
    lora = jnp.concatenate([seg["xw"], seg["xa"], seg["xg"], z(F_OFF - 288), seg["f"],
                            z(LORA_TILE - F_OFF - N_HEADS)], axis=0)
    fpad = jnp.concatenate([seg["f"], z(LANES - N_HEADS)], axis=0)
    wtok = jnp.concatenate([seg["gr"], seg["gf"], seg["r"], seg["k"], seg["v"], lora, seg["q"]], axis=0)
    wt_prompt = jnp.concatenate([seg["kf"], seg["vf"], fpad], axis=0)
    wt_sample = jnp.concatenate([seg["q"], seg["kf"], seg["vf"], fpad], axis=0)
    row = lambda x: x.reshape(1, -1).astype(F32)
    hid = jnp.arange(D_HEADS) // HEAD_DIM
    i128 = jnp.arange(LANES)
    upper_incl = (i128[:, None] <= i128[None, :])
    lower_incl = (i128[:, None] >= i128[None, :])
    ones = jnp.ones((LANES, LANES), bool)
    return dict(
        wtok=wtok.astype(BF16), wt_prompt=wt_prompt.astype(BF16), wt_sample=wt_sample.astype(BF16),
        mu=jnp.pad(row(mu), ((0, 0), (0, RW - RWKV_COLS))),
        w0=row(w0), a0=row(a0), k_k=row(k_k), k_a=row(k_a), r_k=row(r_k), lnx_g=row(lnx_g), lnx_b=row(lnx_b),
        w2p=jnp.pad(w2, ((0, LANES - LORA_W), (0, 0))).astype(BF16),
        a2p=jnp.pad(a2, ((LORA_W, 0), (0, 0))).astype(BF16),
        g2p=jnp.pad(g2, ((0, 2 * LANES - LORA_G), (0, 0))).astype(BF16),
        bones=(hid[:, None] == hid[None, :]).astype(BF16),
        tri=lower_incl.astype(BF16),
        bf_col=jnp.broadcast_to(b_f.reshape(N_HEADS, 1), (N_HEADS, LANES)).astype(F32),
        bf_row=jnp.pad(row(b_f), ((0, 0), (0, LANES - N_HEADS))),
        u_lane=jnp.concatenate([upper_incl, ones], axis=1).astype(BF16),
        l_sub=jnp.concatenate([lower_incl, ones], axis=0).astype(BF16),
        u_suffix=jnp.concatenate([i128[:, None] > i128[None, :], ones], axis=1).astype(BF16),
        w_br=w_br.astype(BF16), w_bf=w_bf.astype(BF16), w_o=w_o.astype(BF16),
        ln1_g=row(ln1_g), ln1_b=row(ln1_b), w_up=w_up.astype(BF16), w_down=w_down.astype(BF16),
        ln2_g=row(ln2_g), ln2_b=row(ln2_b))


def _unpack_shift(rows):
    return rows[:, :RWKV_COLS]


def _prompt_group(x_prompt, par):
    nb, t, _ = x_prompt.shape
    x = x_prompt.reshape(nb * t, D_MODEL)
    tok, tt = _project(x, par["wtok"], par["wt_prompt"], nb, 256)
    out_r, wkv, shift = _rwkv_prompt(tok, nb, par)
    lft, ct, ccol = _fox_gate(tok, tt, nb, par)
    o_f = _fox_attn(tok, tt, ct, ccol, nb, 256)
    h = _merge(x, tok, out_r, o_f, par, 256)
    y = _ffn(h, par, 512)
    kt = tt[:, 0:D_HEADS].reshape(nb, N_HEADS, HEAD_DIM, t)
    vt = tt[:, D_HEADS:2 * D_HEADS].reshape(nb, N_HEADS, HEAD_DIM, t)
    k_out = jnp.transpose(kt, (0, 3, 1, 2))[None]
    v_out = jnp.transpose(vt, (0, 3, 1, 2))[None]
    lf_out = jnp.transpose(lft, (0, 2, 1))[None]
    wkv_out = jnp.transpose(wkv.reshape(nb, N_PAIRS, HEAD_DIM, 2, HEAD_DIM), (0, 1, 3, 2, 4))
    wkv_out = wkv_out.reshape(1, nb, N_HEADS, HEAD_DIM, HEAD_DIM)
    return (y.reshape(nb, t, D_MODEL), k_out, v_out, lf_out, wkv_out,
            _unpack_shift(shift.reshape(nb, RW))[None])


def _sample_group(x_sample, cache_k, cache_v, cache_logf, page_table, state_wkv, state_shift, par):
    nb = x_sample.shape[0]
    x = x_sample.reshape(nb, D_MODEL)
    tok, tt = _project(x, par["wtok"], par["wt_sample"], 1, nb)
    prev = jnp.pad(state_shift, ((0, 0), (0, RW - RWKV_COLS)))
    state_t = jnp.transpose(state_wkv, (1, 2, 3, 0))[None]
    out_r, state_o = _rwkv_sample(tok, prev, state_t, par)
    ck = jnp.transpose(cache_k, (0, 2, 3, 1))[None]
    cv = jnp.transpose(cache_v, (0, 2, 3, 1))[None]
    clf = jnp.transpose(cache_logf, (0, 2, 1))[None]
    o_f, lfn = _fox_decode(tt, ck, cv, clf, page_table, par)
    h = _merge(x, tok, out_r, o_f, par, nb)
    y = _ffn(h, par, nb)
    kt = tt[0, D_HEADS:2 * D_HEADS].reshape(N_HEADS, HEAD_DIM, nb)
    vt = tt[0, 2 * D_HEADS:3 * D_HEADS].reshape(N_HEADS, HEAD_DIM, nb)
    k_out = jnp.transpose(kt, (2, 0, 1)).reshape(1, nb, 1, N_HEADS, HEAD_DIM)
    v_out = jnp.transpose(vt, (2, 0, 1)).reshape(1, nb, 1, N_HEADS, HEAD_DIM)
    lf_out = jnp.transpose(lfn, (1, 0)).reshape(1, nb, 1, N_HEADS)
    wkv_out = jnp.transpose(state_o[0], (3, 0, 1, 2))[None]
    shift_out = tok[:, COL_R:COL_R + RWKV_COLS][None]
    return y.reshape(nb, 1, D_MODEL), k_out, v_out, lf_out, wkv_out, shift_out


def kernel(x_prompt, x_sample, cache_k, cache_v, cache_logf, page_table, state_wkv, state_shift, w_in, mu, w0, w2, a0, a2, g2, k_k, k_a, r_k, lnx_g, lnx_b, b_f, w_br, w_bf, w_o, ln1_g, ln1_b, w_up, w_down, ln2_g, ln2_b):
    assert w_in.shape[0] == 1, "single-layer trunk"
    par = _prepare_params(w_in[0], mu[0], w0[0], w2[0], a0[0], a2[0], g2[0], k_k[0], k_a[0],
                          r_k[0].reshape(-1), lnx_g[0], lnx_b[0], b_f[0], w_br[0], w_bf[0], w_o[0],
                          ln1_g[0], ln1_b[0], w_up[0], w_down[0], ln2_g[0], ln2_b[0])
    yp, kp, vp, lfp, wp, sp = _prompt_group(x_prompt, par)
    ys, kd, vd, lfd, wd, sd = _sample_group(x_sample, cache_k[0], cache_v[0], cache_logf[0], page_table,
                                            state_wkv[0], state_shift[0], par)
    return (yp, ys, kp, vp, lfp, wp, sp, kd, vd, lfd, wd, sd)
```
